```python
import jax, jax.numpy as jnp
from jax import lax
import numpy as np

D_MODEL = 1024
BATCH = 8
SEQ = 4096
DEPTH = 2

GRID_W = 64
CTX_LEN = 256
N_HEADS = 8
N_KV_HEADS = 2
HEAD_DIM = 64
GQA_GROUP = N_HEADS // N_KV_HEADS
ATTN_WIDTH = N_HEADS * HEAD_DIM
KV_WIDTH = N_KV_HEADS * HEAD_DIM
Q_BLOCK = 128
ROPE_THETA = 10000.0
GMLP_GROUPS = 8
GMLP_GROUP_DIM = 64
GMLP_WIDTH = GMLP_GROUPS * GMLP_GROUP_DIM
CHUNK = 128
IN_PROJ_WIDTH = ATTN_WIDTH + 2 * KV_WIDTH + 2 * GMLP_WIDTH
MIX_WIDTH = ATTN_WIDTH + GMLP_WIDTH
POOL_WINDOWS = (2, 4, 8, 16)
POOL_GROUP_DIM = D_MODEL // len(POOL_WINDOWS)
D_FF = ((8 * D_MODEL // 3 + 255) // 256) * 256
N_EVEN = (DEPTH + 1) // 2
N_ODD = DEPTH // 2
EPS = 1e-6

kernel_name = "hybrid_gqa_gmlp_pool_dit_block"


def rms_norm(x, g):
    xf = x.astype(jnp.float32)
    y = xf * lax.rsqrt(jnp.mean(xf * xf, axis=-1, keepdims=True) + EPS)
    return (y * g.astype(jnp.float32)).astype(x.dtype)


def modulate(h, shift, scale):
    return h * (1.0 + scale) + shift


def adaln(cond, w_ada, b_ada):
    m = jax.nn.silu(cond) @ w_ada + b_ada
    return jnp.split(m, 6, axis=-1)


def axial_rope_tables(n):
    rows = n // GRID_W
    row = jnp.repeat(jnp.arange(rows), GRID_W).astype(jnp.float32)
    col = jnp.tile(jnp.arange(GRID_W), rows).astype(jnp.float32)
    half = HEAD_DIM // 2
    freqs = ROPE_THETA ** (-jnp.arange(0, half, 2, dtype=jnp.float32) / half)
    ang = jnp.concatenate([row[:, None] * freqs, col[:, None] * freqs], axis=-1)
    return jnp.cos(ang), jnp.sin(ang)


def apply_rope(x, cos, sin):
    xf = x.astype(jnp.float32)
    x1, x2 = xf[..., 0::2], xf[..., 1::2]
    c = cos[None, :, None, :]
    s = sin[None, :, None, :]
    out = jnp.stack([x1 * c - x2 * s, x1 * s + x2 * c], axis=-1).reshape(x.shape)
    return out.astype(x.dtype)


def attend(q, keys, vals):
    B, N = q.shape[0], q.shape[1]
    scale = HEAD_DIM ** -0.5
    qb = q.reshape(B, N // Q_BLOCK, Q_BLOCK, N_KV_HEADS, GQA_GROUP, HEAD_DIM).transpose(1, 0, 2, 3, 4, 5)

    def block(q_blk):
        s = jnp.einsum('bqkgd,bskd->bkgqs', q_blk, keys, preferred_element_type=jnp.float32) * scale
        p = jax.nn.softmax(s, axis=-1)
        return jnp.einsum('bkgqs,bskd->bqkgd', p.astype(vals.dtype), vals)

    out = lax.map(block, qb)
    return out.transpose(1, 0, 2, 3, 4, 5).reshape(B, N, ATTN_WIDTH)


def spatial_gating(u, v, g_v, w_s, b_s):
    B, N, _ = v.shape
    vg = rms_norm(v.reshape(B, N // CHUNK, CHUNK, GMLP_GROUPS, GMLP_GROUP_DIM), g_v)
    mixed = jnp.einsum('gpq,bnqgd->bnpgd', w_s, vg) + b_s.T[None, None, :, :, None]
    return u * mixed.reshape(B, N, GMLP_WIDTH)


def split_heads(t, h):
    return t.reshape(t.shape[0], t.shape[1], h, HEAD_DIM)


def attn_gmlp_mixer(xn, cn, w_in, w_out, q_norm, k_norm, g_v, w_s, b_s, cos, sin, ctx_live):
    splits = [ATTN_WIDTH, ATTN_WIDTH + KV_WIDTH, ATTN_WIDTH + 2 * KV_WIDTH,
              ATTN_WIDTH + 2 * KV_WIDTH + GMLP_WIDTH]
    qx, kx, vx, ux, gx = jnp.split(xn @ w_in, splits, axis=-1)
    qx = apply_rope(rms_norm(split_heads(qx, N_HEADS), q_norm), cos, sin)
    kx = apply_rope(rms_norm(split_heads(kx, N_KV_HEADS), k_norm), cos, sin)
    vx = split_heads(vx, N_KV_HEADS)
    if ctx_live:
        qc, kc, vc, uc, gc = jnp.split(cn @ w_in, splits, axis=-1)
    else:
        kc, vc = jnp.split(cn @ w_in[:, ATTN_WIDTH:ATTN_WIDTH + 2 * KV_WIDTH], [KV_WIDTH], axis=-1)
    kc = rms_norm(split_heads(kc, N_KV_HEADS), k_norm)
    vc = split_heads(vc, N_KV_HEADS)
    attn_x = attend(qx, jnp.concatenate([kx, kc], axis=1), jnp.concatenate([vx, vc], axis=1))
    gmlp_x = spatial_gating(jax.nn.gelu(ux, approximate=False), jax.nn.gelu(gx, approximate=False), g_v, w_s, b_s)
    out_x = jnp.concatenate([attn_x, gmlp_x], axis=-1) @ w_out
    out_c = None
    if ctx_live:
        qc = rms_norm(split_heads(qc, N_HEADS), q_norm)
        attn_c = attend(qc, kc, vc)
        gmlp_c = spatial_gating(jax.nn.gelu(uc, approximate=False), jax.nn.gelu(gc, approximate=False), g_v, w_s, b_s)
        out_c = jnp.concatenate([attn_c, gmlp_c], axis=-1) @ w_out
    return out_x, out_c


def multiscale_pool(h, w_pool, pool_scale):
    B, N, D = h.shape
    hf = h.astype(jnp.float32)
    cs = jnp.concatenate([jnp.zeros((B, 1, D), jnp.float32), jnp.cumsum(hf, axis=1)], axis=1)
    t = jnp.arange(N)
    outs = []
    for gi, w in enumerate(POOL_WINDOWS):
        left = w // 2
        right = w - 1 - left
        lo = jnp.clip(t - left, 0, N)
        hi = jnp.clip(t + right + 1, 0, N)
        sl = slice(gi * POOL_GROUP_DIM, (gi + 1) * POOL_GROUP_DIM)
        csg = cs[..., sl]
        cnt = (hi - lo).astype(jnp.float32)[None, :, None]
        mean = (jnp.take(csg, hi, axis=1) - jnp.take(csg, lo, axis=1)) / cnt
        outs.append(mean - hf[..., sl])
    pooled = jnp.stack(outs, axis=2).astype(h.dtype)
    y = jnp.einsum('bngc,gcd->bngd', pooled, w_pool).reshape(B, N, D)
    return y * pool_scale


def swiglu(h, w1, w3, w2):
    return (jax.nn.silu(h @ w1) * (h @ w3)) @ w2


def setup_inputs(seed: int = 0) -> dict:
    key = jax.random.key(seed)
    ks = iter(jax.random.split(key, 32))
    f32 = jnp.float32

    def nrm(shape, scale):
        return jax.random.normal(next(ks), shape, f32) * scale

    D = D_MODEL
    return {
        "x": nrm((BATCH, SEQ, D), 1.0),
        "c": nrm((BATCH, D), 1.0),
        "ctx": nrm((BATCH, CTX_LEN, D), 1.0),
        "c_ctx": nrm((D,), 1.0),
        "w_ada": nrm((DEPTH, D, 6 * D), 0.5 * D ** -0.5),
        "b_ada": nrm((DEPTH, 6 * D), 0.02),
        "g_mix": 1.0 + nrm((DEPTH, D), 0.05),
        "g_ffn": 1.0 + nrm((DEPTH, D), 0.05),
        "w_in": nrm((N_EVEN, D, IN_PROJ_WIDTH), D ** -0.5),
        "w_out": nrm((N_EVEN, MIX_WIDTH, D), MIX_WIDTH ** -0.5),
        "q_norm": 1.0 + nrm((N_EVEN, HEAD_DIM), 0.05),
        "k_norm": 1.0 + nrm((N_EVEN, HEAD_DIM), 0.05),
        "gmlp_norm": 1.0 + nrm((N_EVEN, GMLP_GROUPS, GMLP_GROUP_DIM), 0.05),
        "w_spatial": nrm((N_EVEN, GMLP_GROUPS, CHUNK, CHUNK), 0.5 * CHUNK ** -0.5),
        "b_spatial": 1.0 + nrm((N_EVEN, GMLP_GROUPS, CHUNK), 0.1),
        "w_pool": nrm((N_ODD, len(POOL_WINDOWS), POOL_GROUP_DIM, POOL_GROUP_DIM), POOL_GROUP_DIM ** -0.5),
        "pool_scale": 1.0 + nrm((N_ODD, D), 0.1),
        "w1": nrm((DEPTH, D, D_FF), D ** -0.5),
        "w3": nrm((DEPTH, D, D_FF), D ** -0.5),
        "w2": nrm((DEPTH, D_FF, D), D_FF ** -0.5),
        "g_final": 1.0 + nrm((D,), 0.05),
    }


def reference(x, c, ctx, c_ctx, w_ada, b_ada, g_mix, g_ffn, w_in, w_out, q_norm, k_norm,
              gmlp_norm, w_spatial, b_spatial, w_pool, pool_scale, w1, w3, w2, g_final):
    S = x.shape[1]
    cos, sin = axial_rope_tables(S)
    h_ctx = ctx
    for i in range(DEPTH):
        ctx_live = any(j % 2 == 0 for j in range(i + 1, DEPTH))
        sh1, sc1, ga1, sh2, sc2, ga2 = [m[:, None, :] for m in adaln(c, w_ada[i], b_ada[i])]
        csh1, csc1, cga1, csh2, csc2, cga2 = adaln(c_ctx, w_ada[i], b_ada[i])
        xn = modulate(rms_norm(x, g_mix[i]), sh1, sc1)
        if i % 2 == 0:
            e = i // 2
            cn = modulate(rms_norm(h_ctx, g_mix[i]), csh1, csc1)
            mix_x, mix_c = attn_gmlp_mixer(xn, cn, w_in[e], w_out[e], q_norm[e], k_norm[e], gmlp_norm[e],
                                           w_spatial[e], b_spatial[e], cos, sin, ctx_live)
        else:
            o = i // 2
            mix_x = multiscale_pool(xn, w_pool[o], pool_scale[o])
            mix_c = None
            if ctx_live:
                cn = modulate(rms_norm(h_ctx, g_mix[i]), csh1, csc1)
                mix_c = multiscale_pool(cn, w_pool[o], pool_scale[o])
        x = x + ga1 * mix_x
        x = x + ga2 * swiglu(modulate(rms_norm(x, g_ffn[i]), sh2, sc2), w1[i], w3[i], w2[i])
        if ctx_live:
            h_ctx = h_ctx + cga1 * mix_c
            h_ctx = h_ctx + cga2 * swiglu(modulate(rms_norm(h_ctx, g_ffn[i]), csh2, csc2), w1[i], w3[i], w2[i])
    return rms_norm(x, g_final)
```

```python
import functools

import numpy as np
import jax
import jax.numpy as jnp
from jax import lax
from jax.experimental import pallas as pl
from jax.experimental.pallas import tpu as pltpu

D_MODEL = 1024
GRID_W = 64
N_HEADS = 8
N_KV_HEADS = 2
HEAD_DIM = 64
GQA_GROUP = N_HEADS // N_KV_HEADS
ATTN_WIDTH = N_HEADS * HEAD_DIM
KV_WIDTH = N_KV_HEADS * HEAD_DIM
ROPE_THETA = 10000.0
GMLP_GROUPS = 8
GMLP_GROUP_DIM = 64
GMLP_WIDTH = GMLP_GROUPS * GMLP_GROUP_DIM
CHUNK = 128
POOL_WINDOWS = (2, 4, 8, 16)
POOL_GROUP_DIM = D_MODEL // len(POOL_WINDOWS)
EPS = 1e-6

LANES = 128
HALO = 8
COND_ROWS = 16
VMEM_LIMIT = 56 * 1024 * 1024

F32 = jnp.float32
BF16 = jnp.bfloat16


def _const_spec(shape):
    nd = len(shape)
    return pl.BlockSpec(shape, lambda *_: (0,) * nd, pipeline_mode=pl.Buffered(1))


def _params(n_axes):
    return pltpu.CompilerParams(dimension_semantics=("arbitrary",) * n_axes,
                                vmem_limit_bytes=VMEM_LIMIT)


def _rms_rows(x, gain):
    ms = jnp.mean(x * x, axis=-1, keepdims=True)
    return (x * lax.rsqrt(ms + EPS)) * gain


def _seg_mean_sq(t, bsum_ref):
    sq = t * t
    hi = sq.astype(BF16)
    lo = (sq - hi.astype(F32)).astype(BF16)
    ss = jnp.dot(jnp.concatenate([hi, lo], axis=1), bsum_ref[...], preferred_element_type=F32)
    return ss * (1.0 / HEAD_DIM)


def _swiglu(h, w1_ref, w3_ref, w2_ref):
    a = jnp.dot(h, w1_ref[...], preferred_element_type=F32)
    b = jnp.dot(h, w3_ref[...], preferred_element_type=F32)
    g = (a * jax.nn.sigmoid(a) * b).astype(BF16)
    return jnp.dot(g, w2_ref[...], preferred_element_type=F32)


def _adaln_kernel(cond_ref, w_ref, b_ref, o_ref):
    s = cond_ref[...]
    s = (s * jax.nn.sigmoid(s)).astype(BF16)
    o_ref[0] = jnp.dot(s, w_ref[0].astype(BF16), preferred_element_type=F32) + b_ref[0]


def _adaln(cond, w_ada, b_ada, tn=1536):
    depth, d, n = w_ada.shape
    return pl.pallas_call(
        _adaln_kernel,
        grid=(depth, n // tn),
        in_specs=[
            pl.BlockSpec((COND_ROWS, d), lambda l, j: (0, 0)),
            pl.BlockSpec((1, d, tn), lambda l, j: (l, 0, j)),
            pl.BlockSpec((1, 1, tn), lambda l, j: (l, 0, j)),
        ],
        out_specs=pl.BlockSpec((1, COND_ROWS, tn), lambda l, j: (l, 0, j)),
        out_shape=jax.ShapeDtypeStruct((depth, COND_ROWS, n), F32),
        compiler_params=_params(2),
        name="adaln",
    )(cond, w_ada, b_ada.reshape(depth, 1, n))


def _rope(t, cos, sin_signed, even_lane):
    partner = jnp.where(even_lane, pltpu.roll(t, LANES - 1, 1), pltpu.roll(t, 1, 1))
    return t * cos + partner * sin_signed


def _gelu(x):
    return 0.5 * x * (1.0 + lax.erf(x * np.float32(np.sqrt(0.5))))


def _pre0_kernel(x_ref, mod_ref, gmix_ref, win_ref, cos_ref, sin_ref, qg_ref, kg_ref, gv_ref,
                 bsum_ref, ws_ref, bs_ref, q_ref, kt_ref, v_ref, gm_ref):
    tm = x_ref.shape[1]
    m = mod_ref[0]
    h = _rms_rows(x_ref[0], gmix_ref[...]) * (1.0 + m[1:2]) + m[0:1]
    proj = jnp.dot(h.astype(BF16), win_ref[...], preferred_element_type=F32)

    lane = lax.broadcasted_iota(jnp.int32, (tm, LANES), 1)
    even_lane = (lane % 2) == 0
    cos = cos_ref[...]
    sin_signed = sin_ref[...]

    def head_norm_rope(t, gain):
        tn = (t * lax.rsqrt(_seg_mean_sq(t, bsum_ref) + EPS)) * gain
        return _rope(tn, cos, sin_signed, even_lane)

    for g in range(GQA_GROUP):
        t = proj[:, g * LANES:(g + 1) * LANES]
        q_ref[0, :, g * LANES:(g + 1) * LANES] = (
            head_norm_rope(t, qg_ref[...]) * (HEAD_DIM ** -0.5)).astype(BF16)
    k = head_norm_rope(proj[:, ATTN_WIDTH:ATTN_WIDTH + KV_WIDTH], kg_ref[...])
    kt_ref[0] = k.T.astype(BF16)
    v_ref[0] = proj[:, ATTN_WIDTH + KV_WIDTH:ATTN_WIDTH + 2 * KV_WIDTH].astype(BF16)

    u0 = ATTN_WIDTH + 2 * KV_WIDTH
    g0 = u0 + GMLP_WIDTH
    left = lax.broadcasted_iota(jnp.int32, (CHUNK, LANES), 1) < GMLP_GROUP_DIM
    for j in range(GMLP_WIDTH // LANES):
        u = _gelu(proj[:, u0 + j * LANES:u0 + (j + 1) * LANES])
        vv = _gelu(proj[:, g0 + j * LANES:g0 + (j + 1) * LANES])
        vg = (vv * lax.rsqrt(_seg_mean_sq(vv, bsum_ref) + EPS)) * gv_ref[:, j * LANES:(j + 1) * LANES]
        bias = bs_ref[:, j * LANES:(j + 1) * LANES]
        for n in range(tm // CHUNK):
            blk = vg[n * CHUNK:(n + 1) * CHUNK]
            rhs = jnp.concatenate([jnp.where(left, blk, 0.0), jnp.where(left, 0.0, blk)],
                                  axis=0).astype(BF16)
            mixed = jnp.dot(ws_ref[j], rhs, preferred_element_type=F32) + bias
            gm_ref[0, n * CHUNK:(n + 1) * CHUNK, j * LANES:(j + 1) * LANES] = (
                u[n * CHUNK:(n + 1) * CHUNK] * mixed).astype(BF16)


def _pre0(x, mod0, g_mix0, w_in_b, cos_t, sin_t, qg, kg, gv, bsum, ws2, bs_t, tm):
    b, s, d = x.shape
    nw = w_in_b.shape[1]
    return pl.pallas_call(
        _pre0_kernel,
        grid=(b, s // tm),
        in_specs=[
            pl.BlockSpec((1, tm, d), lambda bi, i: (bi, i, 0)),
            pl.BlockSpec((1, 6, d), lambda bi, i: (bi, 0, 0)),
            _const_spec((1, d)),
            _const_spec((d, nw)),
            pl.BlockSpec((tm, LANES), lambda bi, i: (i, 0)),
            pl.BlockSpec((tm, LANES), lambda bi, i: (i, 0)),
            _const_spec((1, LANES)),
            _const_spec((1, LANES)),
            _const_spec((1, GMLP_WIDTH)),
            _const_spec((2 * LANES, LANES)),
            _const_spec(ws2.shape),
            _const_spec(bs_t.shape),
        ],
        out_specs=[
            pl.BlockSpec((1, tm, ATTN_WIDTH), lambda bi, i: (bi, i, 0)),
            pl.BlockSpec((1, KV_WIDTH, tm), lambda bi, i: (bi, 0, i)),
            pl.BlockSpec((1, tm, KV_WIDTH), lambda bi, i: (bi, i, 0)),
            pl.BlockSpec((1, tm, GMLP_WIDTH), lambda bi, i: (bi, i, 0)),
        ],
        out_shape=[
            jax.ShapeDtypeStruct((b, s, ATTN_WIDTH), BF16),
            jax.ShapeDtypeStruct((b, KV_WIDTH, s), BF16),
            jax.ShapeDtypeStruct((b, s, KV_WIDTH), BF16),
            jax.ShapeDtypeStruct((b, s, GMLP_WIDTH), BF16),
        ],
        compiler_params=_params(2),
        name="pre0",
    )(x, mod0, g_mix0, w_in_b, cos_t, sin_t, qg, kg, gv, bsum, ws2, bs_t)


def _ctx_kernel(c_ref, mod_ref, gmix_ref, wkv_ref, kg_ref, bsum_ref, kt_ref, v_ref):
    m = mod_ref[0]
    h = _rms_rows(c_ref[0], gmix_ref[...]) * (1.0 + m[1:2]) + m[0:1]
    proj = jnp.dot(h.astype(BF16), wkv_ref[...], preferred_element_type=F32)
    k = proj[:, :KV_WIDTH]
    k = (k * lax.rsqrt(_seg_mean_sq(k, bsum_ref) + EPS)) * kg_ref[...]
    kt_ref[0] = k.T.astype(BF16)
    v_ref[0] = proj[:, KV_WIDTH:].astype(BF16)


def _ctx_kv(ctx, mod0, g_mix0, w_kv_b, kg, bsum, ctx_row):
    b, n, d = ctx.shape
    return pl.pallas_call(
        _ctx_kernel,
        grid=(b,),
        in_specs=[
            pl.BlockSpec((1, n, d), lambda bi: (bi, 0, 0)),
            pl.BlockSpec((1, 6, d), lambda bi: (ctx_row, 0, 0)),
            _const_spec((1, d)),
            _const_spec((d, 2 * KV_WIDTH)),
            _const_spec((1, LANES)),
            _const_spec((2 * LANES, LANES)),
        ],
        out_specs=[
            pl.BlockSpec((1, KV_WIDTH, n), lambda bi: (bi, 0, 0)),
            pl.BlockSpec((1, n, KV_WIDTH), lambda bi: (bi, 0, 0)),
        ],
        out_shape=[
            jax.ShapeDtypeStruct((b, KV_WIDTH, n), BF16),
            jax.ShapeDtypeStruct((b, n, KV_WIDTH), BF16),
        ],
        compiler_params=_params(1),
        name="ctx_kv",
    )(ctx, mod0, g_mix0, w_kv_b, kg, bsum)


def _attn_kernel(q_ref, kt_ref, v_ref, kct_ref, vc_ref, o_ref):
    tq = q_ref.shape[1]
    left = lax.broadcasted_iota(jnp.int32, (tq, LANES), 1) < HEAD_DIM
    for g in range(GQA_GROUP):
        qc = q_ref[0, :, g * LANES:(g + 1) * LANES].astype(F32)
        outs = []
        for kh in range(N_KV_HEADS):
            qz = jnp.where(left if kh == 0 else jnp.logical_not(left), qc, 0.0).astype(BF16)
            s1 = jnp.dot(qz, kt_ref[0], preferred_element_type=F32)
            s2 = jnp.dot(qz, kct_ref[0], preferred_element_type=F32)
            mx = jnp.maximum(jnp.max(s1, axis=-1, keepdims=True), jnp.max(s2, axis=-1, keepdims=True))
            p1 = jnp.exp(s1 - mx)
            p2 = jnp.exp(s2 - mx)
            den = jnp.sum(p1, axis=-1, keepdims=True) + jnp.sum(p2, axis=-1, keepdims=True)
            o = (jnp.dot(p1.astype(BF16), v_ref[0], preferred_element_type=F32)
                 + jnp.dot(p2.astype(BF16), vc_ref[0], preferred_element_type=F32))
            outs.append(o / den)
        o_ref[0, :, g * LANES:(g + 1) * LANES] = jnp.where(left, outs[0], outs[1]).astype(BF16)


def _attention(q, kt, v, kct, vc, tq):
    b, s, _ = q.shape
    nc = vc.shape[1]
    return pl.pallas_call(
        _attn_kernel,
        grid=(b, s // tq),
        in_specs=[
            pl.BlockSpec((1, tq, ATTN_WIDTH), lambda bi, i: (bi, i, 0)),
            pl.BlockSpec((1, KV_WIDTH, s), lambda bi, i: (bi, 0, 0)),
            pl.BlockSpec((1, s, KV_WIDTH), lambda bi, i: (bi, 0, 0)),
            pl.BlockSpec((1, KV_WIDTH, nc), lambda bi, i: (bi, 0, 0)),
            pl.BlockSpec((1, nc, KV_WIDTH), lambda bi, i: (bi, 0, 0)),
        ],
        out_specs=pl.BlockSpec((1, tq, ATTN_WIDTH), lambda bi, i: (bi, i, 0)),
        out_shape=jax.ShapeDtypeStruct((b, s, ATTN_WIDTH), BF16),
        compiler_params=_params(2),
        name="attention",
    )(q, kt, v, kct, vc)


def _post0_kernel(x_ref, a_ref, gm_ref, mod_ref, gffn_ref, woa_ref, wog_ref, w1_ref, w3_ref, w2_ref,
                  o_ref):
    m = mod_ref[0]
    mix = (jnp.dot(a_ref[0], woa_ref[...], preferred_element_type=F32)
           + jnp.dot(gm_ref[0], wog_ref[...], preferred_element_type=F32))
    x1 = x_ref[0] + m[2:3] * mix
    h = _rms_rows(x1, gffn_ref[...]) * (1.0 + m[4:5]) + m[3:4]
    o_ref[0] = x1 + m[5:6] * _swiglu(h.astype(BF16), w1_ref, w3_ref, w2_ref)


def _post0(x, attn, gm, mod0, g_ffn0, woa, wog, w1, w3, w2, tm):
    b, s, d = x.shape
    dff = w1.shape[1]
    return pl.pallas_call(
        _post0_kernel,
        grid=(b, s // tm),
        in_specs=[
            pl.BlockSpec((1, tm, d), lambda bi, i: (bi, i, 0)),
            pl.BlockSpec((1, tm, ATTN_WIDTH), lambda bi, i: (bi, i, 0)),
            pl.BlockSpec((1, tm, GMLP_WIDTH), lambda bi, i: (bi, i, 0)),
            pl.BlockSpec((1, 6, d), lambda bi, i: (bi, 0, 0)),
            _const_spec((1, d)),
            _const_spec((ATTN_WIDTH, d)),
            _const_spec((GMLP_WIDTH, d)),
            _const_spec((d, dff)),
            _const_spec((d, dff)),
            _const_spec((dff, d)),
        ],
        out_specs=pl.BlockSpec((1, tm, d), lambda bi, i: (bi, i, 0)),
        out_shape=jax.ShapeDtypeStruct((b, s, d), F32),
        compiler_params=_params(2),
        name="post0",
    )(x, attn, gm, mod0, g_ffn0, woa, wog, w1, w3, w2)


def _layer1_kernel(x_ref, prev_ref, next_ref, mod_ref, gmix_ref, gffn_ref, ps_ref, band_ref, wp_ref,
                   w1_ref, w3_ref, w2_ref, gfin_ref, o_ref, *, seq_len):
    tm = x_ref.shape[1]
    i = pl.program_id(1)
    m = mod_ref[0]

    def norm_mod(t):
        return _rms_rows(t, gmix_ref[...]) * (1.0 + m[1:2]) + m[0:1]

    x = x_ref[0]
    xn = norm_mod(x)
    xp = jnp.where(i > 0, norm_mod(prev_ref[0]), 0.0)
    xq = jnp.where(i < pl.num_programs(1) - 1, norm_mod(next_ref[0]), 0.0)
    ext = jnp.concatenate([xp, xn, xq], axis=0)
    ext_hi = ext.astype(BF16)
    ext_lo = (ext - ext_hi.astype(F32)).astype(BF16)

    sub = band_ref.shape[1]
    pos = i * tm + lax.broadcasted_iota(jnp.int32, (tm, POOL_GROUP_DIM), 0)
    ys = []
    for gi, w in enumerate(POOL_WINDOWS):
        left_w = w // 2
        right_w = w - 1 - left_w
        cnt = (jnp.minimum(pos + right_w + 1, seq_len) - jnp.maximum(pos - left_w, 0)).astype(F32)
        sl = slice(gi * POOL_GROUP_DIM, (gi + 1) * POOL_GROUP_DIM)
        sums = []
        for r in range(tm // sub):
            rows = slice(r * sub, r * sub + sub + 2 * HALO)
            sums.append(jnp.dot(band_ref[gi], ext_hi[rows, sl], preferred_element_type=F32)
                        + jnp.dot(band_ref[gi], ext_lo[rows, sl], preferred_element_type=F32))
        pooled = jnp.concatenate(sums, axis=0) / cnt - xn[:, sl]
        ys.append(jnp.dot(pooled.astype(BF16), wp_ref[gi], preferred_element_type=F32))
    y = jnp.concatenate(ys, axis=1) * ps_ref[...]
    x1 = x + m[2:3] * y
    h = _rms_rows(x1, gffn_ref[...]) * (1.0 + m[4:5]) + m[3:4]
    x2 = x1 + m[5:6] * _swiglu(h.astype(BF16), w1_ref, w3_ref, w2_ref)
    o_ref[0] = _rms_rows(x2, gfin_ref[...])


def _layer1(x, mod1, g_mix1, g_ffn1, pool_scale, band, wp, w1, w3, w2, g_final, tm):
    b, s, d = x.shape
    dff = w1.shape[1]
    per = tm // HALO
    last = s // HALO - 1
    return pl.pallas_call(
        functools.partial(_layer1_kernel, seq_len=s),
        grid=(b, s // tm),
        in_specs=[
            pl.BlockSpec((1, tm, d), lambda bi, i: (bi, i, 0)),
            pl.BlockSpec((1, HALO, d), lambda bi, i: (bi, jnp.maximum(i * per - 1, 0), 0)),
            pl.BlockSpec((1, HALO, d), lambda bi, i: (bi, jnp.minimum((i + 1) * per, last), 0)),
            pl.BlockSpec((1, 6, d), lambda bi, i: (bi, 0, 0)),
            _const_spec((1, d)),
            _const_spec((1, d)),
            _const_spec((1, d)),
            _const_spec(band.shape),
            _const_spec(wp.shape),
            _const_spec((d, dff)),
            _const_spec((d, dff)),
            _const_spec((dff, d)),
            _const_spec((1, d)),
        ],
        out_specs=pl.BlockSpec((1, tm, d), lambda bi, i: (bi, i, 0)),
        out_shape=jax.ShapeDtypeStruct((b, s, d), F32),
        compiler_params=_params(2),
        name="layer1",
    )(x, x, x, mod1, g_mix1, g_ffn1, pool_scale, band, wp, w1, w3, w2, g_final)


def _rope_tables(n):
    rows = n // GRID_W
    row = jnp.repeat(jnp.arange(rows), GRID_W).astype(F32)
    col = jnp.tile(jnp.arange(GRID_W), rows).astype(F32)
    half = HEAD_DIM // 2
    freqs = ROPE_THETA ** (-jnp.arange(0, half, 2, dtype=F32) / half)
    ang = jnp.concatenate([row[:, None] * freqs, col[:, None] * freqs], axis=-1)
    cos = jnp.tile(jnp.repeat(jnp.cos(ang), 2, axis=1), (1, LANES // HEAD_DIM))
    sin = jnp.tile(jnp.repeat(jnp.sin(ang), 2, axis=1), (1, LANES // HEAD_DIM))
    sign = jnp.where(jnp.arange(LANES) % 2 == 0, -1.0, 1.0).astype(F32)
    return cos, sin * sign


def _band_matrices(sub):
    t = np.arange(sub)[:, None]
    e = np.arange(sub + 2 * HALO)[None, :]
    mats = []
    for w in POOL_WINDOWS:
        left = w // 2
        right = w - 1 - left
        mats.append(((e >= t + HALO - left) & (e <= t + HALO + right)).astype(np.float32))
    return jnp.asarray(np.stack(mats), dtype=BF16)


def kernel(x, c, ctx, c_ctx, w_ada, b_ada, g_mix, g_ffn, w_in, w_out, q_norm, k_norm, gmlp_norm,
           w_spatial, b_spatial, w_pool, pool_scale, w1, w3, w2, g_final):
    b, s, d = x.shape
    depth = w_ada.shape[0]
    assert depth == 2 and d == D_MODEL and s % CHUNK == 0
    tm_pre, tq, tm_ffn = 256, 256, 512

    cond = jnp.concatenate([c, c_ctx[None], jnp.zeros((COND_ROWS - b - 1, d), F32)], axis=0)
    mods = _adaln(cond, w_ada, b_ada).reshape(depth, COND_ROWS, 6, d)

    wi = w_in[0]
    wq = wi[:, :ATTN_WIDTH].reshape(d, N_KV_HEADS, GQA_GROUP, HEAD_DIM).transpose(0, 2, 1, 3)
    w_in_b = jnp.concatenate([wq.reshape(d, ATTN_WIDTH), wi[:, ATTN_WIDTH:]], axis=1).astype(BF16)
    w_kv_b = wi[:, ATTN_WIDTH:ATTN_WIDTH + 2 * KV_WIDTH].astype(BF16)
    wo = w_out[0]
    woa = wo[:ATTN_WIDTH].reshape(N_KV_HEADS, GQA_GROUP, HEAD_DIM, d).transpose(1, 0, 2, 3)
    woa = woa.reshape(ATTN_WIDTH, d).astype(BF16)
    wog = wo[ATTN_WIDTH:].astype(BF16)
    cos_t, sin_t = _rope_tables(s)
    qg = jnp.tile(q_norm[0], LANES // HEAD_DIM)[None]
    kg = jnp.tile(k_norm[0], LANES // HEAD_DIM)[None]
    gv = gmlp_norm[0].reshape(1, GMLP_WIDTH)
    seg = np.arange(LANES) // HEAD_DIM
    bsum = jnp.asarray(np.tile((seg[:, None] == seg[None, :]).astype(np.float32), (2, 1)), dtype=BF16)
    ws = w_spatial[0]
    ws2 = jnp.concatenate([ws[0::2], ws[1::2]], axis=2).astype(BF16)
    bs_t = jnp.repeat(b_spatial[0].T, GMLP_GROUP_DIM, axis=1)
    w1b, w3b, w2b = w1.astype(BF16), w3.astype(BF16), w2.astype(BF16)

    q, kt, v, gm = _pre0(x, mods[0], g_mix[0][None], w_in_b, cos_t, sin_t, qg, kg, gv, bsum, ws2,
                         bs_t, tm_pre)
    kct, vc = _ctx_kv(ctx, mods[0], g_mix[0][None], w_kv_b, kg, bsum, b)
    attn = _attention(q, kt, v, kct, vc, tq)
    x1 = _post0(x, attn, gm, mods[0], g_ffn[0][None], woa, wog, w1b[0], w3b[0], w2b[0], tm_ffn)

    band = _band_matrices(CHUNK)
    return _layer1(x1, mods[1], g_mix[1][None], g_ffn[1][None], pool_scale[0][None], band,
                   w_pool[0].astype(BF16), w1b[1], w3b[1], w2b[1], g_final[None], tm_ffn)
```

```python
import functools

import numpy as np
import jax
import jax.numpy as jnp
from jax import lax
from jax.experimental import pallas as pl
from jax.experimental.pallas import tpu as pltpu

D_MODEL = 1024
GRID_W = 64
N_HEADS = 8
N_KV_HEADS = 2
HEAD_DIM = 64
GQA_GROUP = N_HEADS // N_KV_HEADS
ATTN_WIDTH = N_HEADS * HEAD_DIM
KV_WIDTH = N_KV_HEADS * HEAD_DIM
ROPE_THETA = 10000.0
GMLP_GROUPS = 8
GMLP_GROUP_DIM = 64
GMLP_WIDTH = GMLP_GROUPS * GMLP_GROUP_DIM
CHUNK = 128
POOL_WINDOWS = (2, 4, 8, 16)
POOL_GROUP_DIM = D_MODEL // len(POOL_WINDOWS)
EPS = 1e-6
Q_SCALE = float(HEAD_DIM ** -0.5 * np.log2(np.e))

LANES = 128
HALO = 8
COND_ROWS = 16
VMEM_LIMIT = 56 * 1024 * 1024
KEY_TILE = 256
MAX_SAFE_SHIFT = 60.0

F32 = jnp.float32
BF16 = jnp.bfloat16


def _const_spec(shape):
    nd = len(shape)
    return pl.BlockSpec(shape, lambda *_: (0,) * nd, pipeline_mode=pl.Buffered(1))


def _params(n_axes):
    return pltpu.CompilerParams(dimension_semantics=("arbitrary",) * n_axes,
                                vmem_limit_bytes=VMEM_LIMIT)


def _rms_rows(x, gain):
    ms = jnp.mean(x * x, axis=-1, keepdims=True)
    return (x * lax.rsqrt(ms + EPS)) * gain


def _seg_mean_sq(t, bsum_ref):
    sq = t * t
    hi = sq.astype(BF16)
    lo = (sq - hi.astype(F32)).astype(BF16)
    ss = jnp.dot(jnp.concatenate([hi, lo], axis=1), bsum_ref[...], preferred_element_type=F32)
    return ss * (1.0 / HEAD_DIM)


def _swiglu(h, w1_ref, w3_ref, w2_ref):
    a = jnp.dot(h, w1_ref[...], preferred_element_type=F32)
    b = jnp.dot(h, w3_ref[...], preferred_element_type=F32)
    g = (a * jax.nn.sigmoid(a) * b).astype(BF16)
    return jnp.dot(g, w2_ref[...], preferred_element_type=F32)


def _adaln_kernel(cond_ref, w_ref, b_ref, o_ref):
    s = cond_ref[...]
    s = (s * jax.nn.sigmoid(s)).astype(BF16)
    o_ref[0] = jnp.dot(s, w_ref[0].astype(BF16), preferred_element_type=F32) + b_ref[0]


def _adaln(cond, w_ada, b_ada, tn=1536):
    depth, d, n = w_ada.shape
    return pl.pallas_call(
        _adaln_kernel,
        grid=(depth, n // tn),
        in_specs=[
            pl.BlockSpec((COND_ROWS, d), lambda l, j: (0, 0)),
            pl.BlockSpec((1, d, tn), lambda l, j: (l, 0, j)),
            pl.BlockSpec((1, 1, tn), lambda l, j: (l, 0, j)),
        ],
        out_specs=pl.BlockSpec((1, COND_ROWS, tn), lambda l, j: (l, 0, j)),
        out_shape=jax.ShapeDtypeStruct((depth, COND_ROWS, n), F32),
        compiler_params=_params(2),
        name="adaln",
    )(cond, w_ada, b_ada.reshape(depth, 1, n))


def _rope(t, cos, sin_signed, even_lane):
    partner = jnp.where(even_lane, pltpu.roll(t, LANES - 1, 1), pltpu.roll(t, 1, 1))
    return t * cos + partner * sin_signed


def _gelu(x):
    return 0.5 * x * (1.0 + lax.erf(x * np.float32(np.sqrt(0.5))))


def _store_values_with_ones(v, ve_ref):
    ve_ref[0, :, :LANES] = v.astype(BF16)
    ve_ref[0, :, LANES:] = jnp.ones(v.shape, BF16)


def _pre0_kernel(x_ref, mod_ref, gmix_ref, win_ref, cos_ref, sin_ref, qg_ref, kg_ref, gv_ref,
                 bsum_ref, ws_ref, bs_ref, q_ref, kt_ref, v_ref, gm_ref):
    tm = x_ref.shape[1]
    m = mod_ref[0]
    h = _rms_rows(x_ref[0], gmix_ref[...]) * (1.0 + m[1:2]) + m[0:1]
    proj = jnp.dot(h.astype(BF16), win_ref[...], preferred_element_type=F32)

    lane = lax.broadcasted_iota(jnp.int32, (tm, LANES), 1)
    even_lane = (lane % 2) == 0
    cos = cos_ref[...]
    sin_signed = sin_ref[...]

    def head_norm_rope(t, gain):
        tn = (t * lax.rsqrt(_seg_mean_sq(t, bsum_ref) + EPS)) * gain
        return _rope(tn, cos, sin_signed, even_lane)

    for g in range(GQA_GROUP):
        t = proj[:, g * LANES:(g + 1) * LANES]
        q_ref[0, :, g * LANES:(g + 1) * LANES] = (head_norm_rope(t, qg_ref[...]) * Q_SCALE).astype(BF16)
    k = head_norm_rope(proj[:, ATTN_WIDTH:ATTN_WIDTH + KV_WIDTH], kg_ref[...])
    kt_ref[0] = k.T.astype(BF16)
    _store_values_with_ones(proj[:, ATTN_WIDTH + KV_WIDTH:ATTN_WIDTH + 2 * KV_WIDTH], v_ref)

    u0 = ATTN_WIDTH + 2 * KV_WIDTH
    g0 = u0 + GMLP_WIDTH
    left = lax.broadcasted_iota(jnp.int32, (CHUNK, LANES), 1) < GMLP_GROUP_DIM
    for j in range(GMLP_WIDTH // LANES):
        u = _gelu(proj[:, u0 + j * LANES:u0 + (j + 1) * LANES])
        vv = _gelu(proj[:, g0 + j * LANES:g0 + (j + 1) * LANES])
        vg = (vv * lax.rsqrt(_seg_mean_sq(vv, bsum_ref) + EPS)) * gv_ref[:, j * LANES:(j + 1) * LANES]
        bias = bs_ref[:, j * LANES:(j + 1) * LANES]
        for n in range(tm // CHUNK):
            blk = vg[n * CHUNK:(n + 1) * CHUNK]
            rhs = jnp.concatenate([jnp.where(left, blk, 0.0), jnp.where(left, 0.0, blk)],
                                  axis=0).astype(BF16)
            mixed = jnp.dot(ws_ref[j], rhs, preferred_element_type=F32) + bias
            gm_ref[0, n * CHUNK:(n + 1) * CHUNK, j * LANES:(j + 1) * LANES] = (
                u[n * CHUNK:(n + 1) * CHUNK] * mixed).astype(BF16)


def _pre0(x, mod0, g_mix0, w_in_b, cos_t, sin_t, qg, kg, gv, bsum, ws2, bs_t, tm):
    b, s, d = x.shape
    nw = w_in_b.shape[1]
    return pl.pallas_call(
        _pre0_kernel,
        grid=(b, s // tm),
        in_specs=[
            pl.BlockSpec((1, tm, d), lambda bi, i: (bi, i, 0)),
            pl.BlockSpec((1, 6, d), lambda bi, i: (bi, 0, 0)),
            _const_spec((1, d)),
            _const_spec((d, nw)),
            pl.BlockSpec((tm, LANES), lambda bi, i: (i, 0)),
            pl.BlockSpec((tm, LANES), lambda bi, i: (i, 0)),
            _const_spec((1, LANES)),
            _const_spec((1, LANES)),
            _const_spec((1, GMLP_WIDTH)),
            _const_spec((2 * LANES, LANES)),
            _const_spec(ws2.shape),
            _const_spec(bs_t.shape),
        ],
        out_specs=[
            pl.BlockSpec((1, tm, ATTN_WIDTH), lambda bi, i: (bi, i, 0)),
            pl.BlockSpec((1, KV_WIDTH, tm), lambda bi, i: (bi, 0, i)),
            pl.BlockSpec((1, tm, 2 * LANES), lambda bi, i: (bi, i, 0)),
            pl.BlockSpec((1, tm, GMLP_WIDTH), lambda bi, i: (bi, i, 0)),
        ],
        out_shape=[
            jax.ShapeDtypeStruct((b, s, ATTN_WIDTH), BF16),
            jax.ShapeDtypeStruct((b, KV_WIDTH, s), BF16),
            jax.ShapeDtypeStruct((b, s, 2 * LANES), BF16),
            jax.ShapeDtypeStruct((b, s, GMLP_WIDTH), BF16),
        ],
        compiler_params=_params(2),
        name="pre0",
    )(x, mod0, g_mix0, w_in_b, cos_t, sin_t, qg, kg, gv, bsum, ws2, bs_t)


def _ctx_kernel(c_ref, mod_ref, gmix_ref, wkv_ref, kg_ref, bsum_ref, kt_ref, v_ref):
    m = mod_ref[0]
    h = _rms_rows(c_ref[0], gmix_ref[...]) * (1.0 + m[1:2]) + m[0:1]
    proj = jnp.dot(h.astype(BF16), wkv_ref[...], preferred_element_type=F32)
    k = proj[:, :KV_WIDTH]
    k = (k * lax.rsqrt(_seg_mean_sq(k, bsum_ref) + EPS)) * kg_ref[...]
    kt_ref[0] = k.T.astype(BF16)
    _store_values_with_ones(proj[:, KV_WIDTH:], v_ref)


def _ctx_kv(ctx, mod0, g_mix0, w_kv_b, kg, bsum, ctx_row):
    b, n, d = ctx.shape
    return pl.pallas_call(
        _ctx_kernel,
        grid=(b,),
        in_specs=[
            pl.BlockSpec((1, n, d), lambda bi: (bi, 0, 0)),
            pl.BlockSpec((1, 6, d), lambda bi: (ctx_row, 0, 0)),
            _const_spec((1, d)),
            _const_spec((d, 2 * KV_WIDTH)),
            _const_spec((1, LANES)),
            _const_spec((2 * LANES, LANES)),
        ],
        out_specs=[
            pl.BlockSpec((1, KV_WIDTH, n), lambda bi: (bi, 0, 0)),
            pl.BlockSpec((1, n, 2 * LANES), lambda bi: (bi, 0, 0)),
        ],
        out_shape=[
            jax.ShapeDtypeStruct((b, KV_WIDTH, n), BF16),
            jax.ShapeDtypeStruct((b, n, 2 * LANES), BF16),
        ],
        compiler_params=_params(1),
        name="ctx_kv",
    )(ctx, mod0, g_mix0, w_kv_b, kg, bsum)


def _attn_kernel(q_ref, kt_ref, ve_ref, o_ref, qz_ref, m_ref, r_ref, k2max_ref):
    tq = q_ref.shape[1]
    nk = kt_ref.shape[2]
    n_heads = GQA_GROUP * N_KV_HEADS
    left = lax.broadcasted_iota(jnp.int32, (tq, LANES), 1) < HEAD_DIM

    @pl.when(pl.program_id(1) == 0)
    def _():
        k2 = jnp.square(kt_ref[0].astype(F32))
        for kh in range(N_KV_HEADS):
            col = jnp.sum(k2[kh * HEAD_DIM:(kh + 1) * HEAD_DIM], axis=0, keepdims=True)
            k2max_ref[kh:kh + 1, :] = jnp.broadcast_to(jnp.max(col, axis=1, keepdims=True), (1, LANES))

    ones = jnp.ones((LANES, LANES), BF16)
    worst = jnp.zeros((tq, LANES), F32)
    for g in range(GQA_GROUP):
        qc = q_ref[0, :, g * LANES:(g + 1) * LANES].astype(F32)
        for kh in range(N_KV_HEADS):
            u = N_KV_HEADS * g + kh
            qz = jnp.where(left, qc, 0.0) if kh == 0 else jnp.where(left, 0.0, qc)
            qz_ref[u] = qz.astype(BF16)
            hi = (qz * qz).astype(BF16)
            lo = (qz * qz - hi.astype(F32)).astype(BF16)
            q2 = (jnp.dot(hi, ones, preferred_element_type=F32)
                  + jnp.dot(lo, ones, preferred_element_type=F32))
            m = jnp.sqrt(q2 * k2max_ref[kh:kh + 1, :])
            m_ref[u] = m
            worst = jnp.maximum(worst, m)

    def finish(r, u):
        r_ref[u] = r[:, :LANES] / r[:, LANES:]

    def streamed():
        for u in range(n_heads):
            qz = qz_ref[u]
            m = m_ref[u]
            m2 = jnp.concatenate([m] * (KEY_TILE // LANES), axis=1)
            r = None
            for j in range(nk // KEY_TILE):
                s = jnp.dot(qz, kt_ref[0, :, j * KEY_TILE:(j + 1) * KEY_TILE], preferred_element_type=F32)
                p = jnp.exp2(s - m2).astype(BF16)
                d = jnp.dot(p, ve_ref[0, j * KEY_TILE:(j + 1) * KEY_TILE, :], preferred_element_type=F32)
                r = d if r is None else r + d
            finish(r, u)

    def exact_max():
        def head(u, carry):
            s = jnp.dot(qz_ref[u], kt_ref[0], preferred_element_type=F32)
            p = jnp.exp2(s - jnp.max(s, axis=-1, keepdims=True)).astype(BF16)
            finish(jnp.dot(p, ve_ref[0], preferred_element_type=F32), u)
            return carry
        lax.fori_loop(0, n_heads, head, 0)

    lax.cond(jnp.max(worst) <= MAX_SAFE_SHIFT, streamed, exact_max)

    for g in range(GQA_GROUP):
        o_ref[0, :, g * LANES:(g + 1) * LANES] = jnp.where(
            left, r_ref[N_KV_HEADS * g], r_ref[N_KV_HEADS * g + 1]).astype(BF16)


def _attention(q, kt, ve, tq):
    b, s, _ = q.shape
    nk = kt.shape[2]
    n_heads = GQA_GROUP * N_KV_HEADS
    assert nk % KEY_TILE == 0
    return pl.pallas_call(
        _attn_kernel,
        grid=(b, s // tq),
        in_specs=[
            pl.BlockSpec((1, tq, ATTN_WIDTH), lambda bi, i: (bi, i, 0)),
            pl.BlockSpec((1, KV_WIDTH, nk), lambda bi, i: (bi, 0, 0)),
            pl.BlockSpec((1, nk, 2 * LANES), lambda bi, i: (bi, 0, 0)),
        ],
        out_specs=pl.BlockSpec((1, tq, ATTN_WIDTH), lambda bi, i: (bi, i, 0)),
        out_shape=jax.ShapeDtypeStruct((b, s, ATTN_WIDTH), BF16),
        scratch_shapes=[
            pltpu.VMEM((n_heads, tq, LANES), BF16),
            pltpu.VMEM((n_heads, tq, LANES), F32),
            pltpu.VMEM((n_heads, tq, LANES), F32),
            pltpu.VMEM((8, LANES), F32),
        ],
        compiler_params=_params(2),
        name="attention",
    )(q, kt, ve)


def _post0_kernel(x_ref, a_ref, gm_ref, mod_ref, gffn_ref, woa_ref, wog_ref, w1_ref, w3_ref, w2_ref,
                  o_ref):
    m = mod_ref[0]
    mix = (jnp.dot(a_ref[0], woa_ref[...], preferred_element_type=F32)
           + jnp.dot(gm_ref[0], wog_ref[...], preferred_element_type=F32))
    x1 = x_ref[0] + m[2:3] * mix
    h = _rms_rows(x1, gffn_ref[...]) * (1.0 + m[4:5]) + m[3:4]
    o_ref[0] = x1 + m[5:6] * _swiglu(h.astype(BF16), w1_ref, w3_ref, w2_ref)


def _post0(x, attn, gm, mod0, g_ffn0, woa, wog, w1, w3, w2, tm):
    b, s, d = x.shape
    dff = w1.shape[1]
    return pl.pallas_call(
        _post0_kernel,
        grid=(b, s // tm),
        in_specs=[
            pl.BlockSpec((1, tm, d), lambda bi, i: (bi, i, 0)),
            pl.BlockSpec((1, tm, ATTN_WIDTH), lambda bi, i: (bi, i, 0)),
            pl.BlockSpec((1, tm, GMLP_WIDTH), lambda bi, i: (bi, i, 0)),
            pl.BlockSpec((1, 6, d), lambda bi, i: (bi, 0, 0)),
            _const_spec((1, d)),
            _const_spec((ATTN_WIDTH, d)),
            _const_spec((GMLP_WIDTH, d)),
            _const_spec((d, dff)),
            _const_spec((d, dff)),
            _const_spec((dff, d)),
        ],
        out_specs=pl.BlockSpec((1, tm, d), lambda bi, i: (bi, i, 0)),
        out_shape=jax.ShapeDtypeStruct((b, s, d), F32),
        compiler_params=_params(2),
        name="post0",
    )(x, attn, gm, mod0, g_ffn0, woa, wog, w1, w3, w2)


def _layer1_kernel(x_ref, prev_ref, next_ref, mod_ref, gmix_ref, gffn_ref, ps_ref, band_ref, wp_ref,
                   w1_ref, w3_ref, w2_ref, gfin_ref, o_ref, *, seq_len):
    tm = x_ref.shape[1]
    i = pl.program_id(1)
    m = mod_ref[0]

    def norm_mod(t):
        return _rms_rows(t, gmix_ref[...]) * (1.0 + m[1:2]) + m[0:1]

    x = x_ref[0]
    xn = norm_mod(x)
    xp = jnp.where(i > 0, norm_mod(prev_ref[0]), 0.0)
    xq = jnp.where(i < pl.num_programs(1) - 1, norm_mod(next_ref[0]), 0.0)
    ext = jnp.concatenate([xp, xn, xq], axis=0)
    ext_hi = ext.astype(BF16)
    ext_lo = (ext - ext_hi.astype(F32)).astype(BF16)

    sub = band_ref.shape[1]
    pos = i * tm + lax.broadcasted_iota(jnp.int32, (tm, POOL_GROUP_DIM), 0)
    ys = []
    for gi, w in enumerate(POOL_WINDOWS):
        left_w = w // 2
        right_w = w - 1 - left_w
        cnt = (jnp.minimum(pos + right_w + 1, seq_len) - jnp.maximum(pos - left_w, 0)).astype(F32)
        sl = slice(gi * POOL_GROUP_DIM, (gi + 1) * POOL_GROUP_DIM)
        sums = []
        for r in range(tm // sub):
            rows = slice(r * sub, r * sub + sub + 2 * HALO)
            sums.append(jnp.dot(band_ref[gi], ext_hi[rows, sl], preferred_element_type=F32)
                        + jnp.dot(band_ref[gi], ext_lo[rows, sl], preferred_element_type=F32))
        pooled = jnp.concatenate(sums, axis=0) / cnt - xn[:, sl]
        ys.append(jnp.dot(pooled.astype(BF16), wp_ref[gi], preferred_element_type=F32))
    y = jnp.concatenate(ys, axis=1) * ps_ref[...]
    x1 = x + m[2:3] * y
    h = _rms_rows(x1, gffn_ref[...]) * (1.0 + m[4:5]) + m[3:4]
    x2 = x1 + m[5:6] * _swiglu(h.astype(BF16), w1_ref, w3_ref, w2_ref)
    o_ref[0] = _rms_rows(x2, gfin_ref[...])


def _layer1(x, mod1, g_mix1, g_ffn1, pool_scale, band, wp, w1, w3, w2, g_final, tm):
    b, s, d = x.shape
    dff = w1.shape[1]
    per = tm // HALO
    last = s // HALO - 1
    return pl.pallas_call(
        functools.partial(_layer1_kernel, seq_len=s),
        grid=(b, s // tm),
        in_specs=[
            pl.BlockSpec((1, tm, d), lambda bi, i: (bi, i, 0)),
            pl.BlockSpec((1, HALO, d), lambda bi, i: (bi, jnp.maximum(i * per - 1, 0), 0)),
            pl.BlockSpec((1, HALO, d), lambda bi, i: (bi, jnp.minimum((i + 1) * per, last), 0)),
            pl.BlockSpec((1, 6, d), lambda bi, i: (bi, 0, 0)),
            _const_spec((1, d)),
            _const_spec((1, d)),
            _const_spec((1, d)),
            _const_spec(band.shape),
            _const_spec(wp.shape),
            _const_spec((d, dff)),
            _const_spec((d, dff)),
            _const_spec((dff, d)),
            _const_spec((1, d)),
        ],
        out_specs=pl.BlockSpec((1, tm, d), lambda bi, i: (bi, i, 0)),
        out_shape=jax.ShapeDtypeStruct((b, s, d), F32),
        compiler_params=_params(2),
        name="layer1",
    )(x, x, x, mod1, g_mix1, g_ffn1, pool_scale, band, wp, w1, w3, w2, g_final)


def _rope_tables(n):
    rows = n // GRID_W
    row = jnp.repeat(jnp.arange(rows), GRID_W).astype(F32)
    col = jnp.tile(jnp.arange(GRID_W), rows).astype(F32)
    half = HEAD_DIM // 2
    freqs = ROPE_THETA ** (-jnp.arange(0, half, 2, dtype=F32) / half)
    ang = jnp.concatenate([row[:, None] * freqs, col[:, None] * freqs], axis=-1)
    cos = jnp.tile(jnp.repeat(jnp.cos(ang), 2, axis=1), (1, LANES // HEAD_DIM))
    sin = jnp.tile(jnp.repeat(jnp.sin(ang), 2, axis=1), (1, LANES // HEAD_DIM))
    sign = jnp.where(jnp.arange(LANES) % 2 == 0, -1.0, 1.0).astype(F32)
    return cos, sin * sign


def _band_matrices(sub):
    t = np.arange(sub)[:, None]
    e = np.arange(sub + 2 * HALO)[None, :]
    mats = []
    for w in POOL_WINDOWS:
        left = w // 2
        right = w - 1 - left
        mats.append(((e >= t + HALO - left) & (e <= t + HALO + right)).astype(np.float32))
    return jnp.asarray(np.stack(mats), dtype=BF16)


def kernel(x, c, ctx, c_ctx, w_ada, b_ada, g_mix, g_ffn, w_in, w_out, q_norm, k_norm, gmlp_norm,
           w_spatial, b_spatial, w_pool, pool_scale, w1, w3, w2, g_final):
    b, s, d = x.shape
    depth = w_ada.shape[0]
    assert depth == 2 and d == D_MODEL and s % CHUNK == 0
    tm_pre, tq, tm_ffn = 256, 512, 512

    cond = jnp.concatenate([c, c_ctx[None], jnp.zeros((COND_ROWS - b - 1, d), F32)], axis=0)
    mods = _adaln(cond, w_ada, b_ada).reshape(depth, COND_ROWS, 6, d)

    wi = w_in[0]
    wq = wi[:, :ATTN_WIDTH].reshape(d, N_KV_HEADS, GQA_GROUP, HEAD_DIM).transpose(0, 2, 1, 3)
    w_in_b = jnp.concatenate([wq.reshape(d, ATTN_WIDTH), wi[:, ATTN_WIDTH:]], axis=1).astype(BF16)
    w_kv_b = wi[:, ATTN_WIDTH:ATTN_WIDTH + 2 * KV_WIDTH].astype(BF16)
    wo = w_out[0]
    woa = wo[:ATTN_WIDTH].reshape(N_KV_HEADS, GQA_GROUP, HEAD_DIM, d).transpose(1, 0, 2, 3)
    woa = woa.reshape(ATTN_WIDTH, d).astype(BF16)
    wog = wo[ATTN_WIDTH:].astype(BF16)
    cos_t, sin_t = _rope_tables(s)
    qg = jnp.tile(q_norm[0], LANES // HEAD_DIM)[None]
    kg = jnp.tile(k_norm[0], LANES // HEAD_DIM)[None]
    gv = gmlp_norm[0].reshape(1, GMLP_WIDTH)
    seg = np.arange(LANES) // HEAD_DIM
    bsum = jnp.asarray(np.tile((seg[:, None] == seg[None, :]).astype(np.float32), (2, 1)), dtype=BF16)
    ws = w_spatial[0]
    ws2 = jnp.concatenate([ws[0::2], ws[1::2]], axis=2).astype(BF16)
    bs_t = jnp.repeat(b_spatial[0].T, GMLP_GROUP_DIM, axis=1)
    w1b, w3b, w2b = w1.astype(BF16), w3.astype(BF16), w2.astype(BF16)

    q, kt, v, gm = _pre0(x, mods[0], g_mix[0][None], w_in_b, cos_t, sin_t, qg, kg, gv, bsum, ws2,
                         bs_t, tm_pre)
    kct, vc = _ctx_kv(ctx, mods[0], g_mix[0][None], w_kv_b, kg, bsum, b)
    attn = _attention(q, jnp.concatenate([kt, kct], axis=2), jnp.concatenate([v, vc], axis=1), tq)
    x1 = _post0(x, attn, gm, mods[0], g_ffn[0][None], woa, wog, w1b[0], w3b[0], w2b[0], tm_ffn)

    band = _band_matrices(CHUNK)
    return _layer1(x1, mods[1], g_mix[1][None], g_ffn[1][None], pool_scale[0][None], band,
                   w_pool[0].astype(BF16), w1b[1], w3b[1], w2b[1], g_final[None], tm_ffn)
```

```python
import functools

import numpy as np
import jax
import jax.numpy as jnp
from jax import lax
from jax.experimental import pallas as pl
from jax.experimental.pallas import tpu as pltpu

D_MODEL = 1024
GRID_W = 64
N_HEADS = 8
N_KV_HEADS = 2
HEAD_DIM = 64
GQA_GROUP = N_HEADS // N_KV_HEADS
ATTN_WIDTH = N_HEADS * HEAD_DIM
KV_WIDTH = N_KV_HEADS * HEAD_DIM
ROPE_THETA = 10000.0
GMLP_GROUPS = 8
GMLP_GROUP_DIM = 64
GMLP_WIDTH = GMLP_GROUPS * GMLP_GROUP_DIM
CHUNK = 128
POOL_WINDOWS = (2, 4, 8, 16)
POOL_GROUP_DIM = D_MODEL // len(POOL_WINDOWS)
EPS = 1e-6
Q_SCALE = float(HEAD_DIM ** -0.5 * np.log2(np.e))

LANES = 128
HALO = 8
COND_ROWS = 16
VMEM_LIMIT = 56 * 1024 * 1024
KEY_TILE = 256
MAX_SAFE_SHIFT = 60.0

F32 = jnp.float32
BF16 = jnp.bfloat16


def _const_spec(shape):
    nd = len(shape)
    return pl.BlockSpec(shape, lambda *_: (0,) * nd, pipeline_mode=pl.Buffered(1))


def _params(n_axes):
    return pltpu.CompilerParams(dimension_semantics=("arbitrary",) * n_axes,
                                vmem_limit_bytes=VMEM_LIMIT)


def _rms_rows(x, gain):
    ms = jnp.mean(x * x, axis=-1, keepdims=True)
    return (x * lax.rsqrt(ms + EPS)) * gain


def _seg_mean_sq(t, bsum_ref):
    sq = t * t
    hi = sq.astype(BF16)
    lo = (sq - hi.astype(F32)).astype(BF16)
    ss = jnp.dot(jnp.concatenate([hi, lo], axis=1), bsum_ref[...], preferred_element_type=F32)
    return ss * (1.0 / HEAD_DIM)


def _swiglu(h, w1_ref, w3_ref, w2_ref):
    a = jnp.dot(h, w1_ref[...], preferred_element_type=F32)
    b = jnp.dot(h, w3_ref[...], preferred_element_type=F32)
    g = (a * jax.nn.sigmoid(a) * b).astype(BF16)
    return jnp.dot(g, w2_ref[...], preferred_element_type=F32)


def _adaln_kernel(cond_ref, w_ref, b_ref, o_ref):
    s = cond_ref[...]
    s = (s * jax.nn.sigmoid(s)).astype(BF16)
    o_ref[0] = jnp.dot(s, w_ref[0].astype(BF16), preferred_element_type=F32) + b_ref[0]


def _adaln(cond, w_ada, b_ada, tn=1536):
    depth, d, n = w_ada.shape
    return pl.pallas_call(
        _adaln_kernel,
        grid=(depth, n // tn),
        in_specs=[
            pl.BlockSpec((COND_ROWS, d), lambda l, j: (0, 0)),
            pl.BlockSpec((1, d, tn), lambda l, j: (l, 0, j)),
            pl.BlockSpec((1, 1, tn), lambda l, j: (l, 0, j)),
        ],
        out_specs=pl.BlockSpec((1, COND_ROWS, tn), lambda l, j: (l, 0, j)),
        out_shape=jax.ShapeDtypeStruct((depth, COND_ROWS, n), F32),
        compiler_params=_params(2),
        name="adaln",
    )(cond, w_ada, b_ada.reshape(depth, 1, n))


def _rope(t, cos, sin_signed, even_lane):
    partner = jnp.where(even_lane, pltpu.roll(t, LANES - 1, 1), pltpu.roll(t, 1, 1))
    return t * cos + partner * sin_signed


def _gelu(x):
    return 0.5 * x * (1.0 + lax.erf(x * np.float32(np.sqrt(0.5))))


def _store_values_with_ones(v, ve_ref):
    ve_ref[0, :, :LANES] = v.astype(BF16)
    ve_ref[0, :, LANES:] = jnp.ones(v.shape, BF16)


def _pre0_kernel(x_ref, mod_ref, gmix_ref, win_ref, cos_ref, sin_ref, qg_ref, kg_ref, gv_ref,
                 bsum_ref, ws_ref, bs_ref, q_ref, kt_ref, v_ref, gm_ref):
    tm = x_ref.shape[1]
    m = mod_ref[0]
    h = _rms_rows(x_ref[0], gmix_ref[...]) * (1.0 + m[1:2]) + m[0:1]
    proj = jnp.dot(h.astype(BF16), win_ref[...], preferred_element_type=F32)

    lane = lax.broadcasted_iota(jnp.int32, (tm, LANES), 1)
    even_lane = (lane % 2) == 0
    cos = cos_ref[...]
    sin_signed = sin_ref[...]

    def head_norm_rope(t, gain):
        tn = (t * lax.rsqrt(_seg_mean_sq(t, bsum_ref) + EPS)) * gain
        return _rope(tn, cos, sin_signed, even_lane)

    for g in range(GQA_GROUP):
        t = proj[:, g * LANES:(g + 1) * LANES]
        q_ref[0, :, g * LANES:(g + 1) * LANES] = (head_norm_rope(t, qg_ref[...]) * Q_SCALE).astype(BF16)
    k = head_norm_rope(proj[:, ATTN_WIDTH:ATTN_WIDTH + KV_WIDTH], kg_ref[...])
    kt_ref[0] = k.T.astype(BF16)
    _store_values_with_ones(proj[:, ATTN_WIDTH + KV_WIDTH:ATTN_WIDTH + 2 * KV_WIDTH], v_ref)

    u0 = ATTN_WIDTH + 2 * KV_WIDTH
    g0 = u0 + GMLP_WIDTH
    left = lax.broadcasted_iota(jnp.int32, (CHUNK, LANES), 1) < GMLP_GROUP_DIM
    for j in range(GMLP_WIDTH // LANES):
        u = _gelu(proj[:, u0 + j * LANES:u0 + (j + 1) * LANES])
        vv = _gelu(proj[:, g0 + j * LANES:g0 + (j + 1) * LANES])
        vg = (vv * lax.rsqrt(_seg_mean_sq(vv, bsum_ref) + EPS)) * gv_ref[:, j * LANES:(j + 1) * LANES]
        bias = bs_ref[:, j * LANES:(j + 1) * LANES]
        for n in range(tm // CHUNK):
            blk = vg[n * CHUNK:(n + 1) * CHUNK]
            rhs = jnp.concatenate([jnp.where(left, blk, 0.0), jnp.where(left, 0.0, blk)],
                                  axis=0).astype(BF16)
            mixed = jnp.dot(ws_ref[j], rhs, preferred_element_type=F32) + bias
            gm_ref[0, n * CHUNK:(n + 1) * CHUNK, j * LANES:(j + 1) * LANES] = (
                u[n * CHUNK:(n + 1) * CHUNK] * mixed).astype(BF16)


def _pre0(x, mod0, g_mix0, w_in_b, cos_t, sin_t, qg, kg, gv, bsum, ws2, bs_t, tm):
    b, s, d = x.shape
    nw = w_in_b.shape[1]
    return pl.pallas_call(
        _pre0_kernel,
        grid=(b, s // tm),
        in_specs=[
            pl.BlockSpec((1, tm, d), lambda bi, i: (bi, i, 0)),
            pl.BlockSpec((1, 6, d), lambda bi, i: (bi, 0, 0)),
            _const_spec((1, d)),
            _const_spec((d, nw)),
            pl.BlockSpec((tm, LANES), lambda bi, i: (i, 0)),
            pl.BlockSpec((tm, LANES), lambda bi, i: (i, 0)),
            _const_spec((1, LANES)),
            _const_spec((1, LANES)),
            _const_spec((1, GMLP_WIDTH)),
            _const_spec((2 * LANES, LANES)),
            _const_spec(ws2.shape),
            _const_spec(bs_t.shape),
        ],
        out_specs=[
            pl.BlockSpec((1, tm, ATTN_WIDTH), lambda bi, i: (bi, i, 0)),
            pl.BlockSpec((1, KV_WIDTH, tm), lambda bi, i: (bi, 0, i)),
            pl.BlockSpec((1, tm, 2 * LANES), lambda bi, i: (bi, i, 0)),
            pl.BlockSpec((1, tm, GMLP_WIDTH), lambda bi, i: (bi, i, 0)),
        ],
        out_shape=[
            jax.ShapeDtypeStruct((b, s, ATTN_WIDTH), BF16),
            jax.ShapeDtypeStruct((b, KV_WIDTH, s), BF16),
            jax.ShapeDtypeStruct((b, s, 2 * LANES), BF16),
            jax.ShapeDtypeStruct((b, s, GMLP_WIDTH), BF16),
        ],
        compiler_params=_params(2),
        name="pre0",
    )(x, mod0, g_mix0, w_in_b, cos_t, sin_t, qg, kg, gv, bsum, ws2, bs_t)


def _ctx_kernel(c_ref, mod_ref, gmix_ref, wkv_ref, kg_ref, bsum_ref, kt_ref, v_ref):
    m = mod_ref[0]
    h = _rms_rows(c_ref[0], gmix_ref[...]) * (1.0 + m[1:2]) + m[0:1]
    proj = jnp.dot(h.astype(BF16), wkv_ref[...], preferred_element_type=F32)
    k = proj[:, :KV_WIDTH]
    k = (k * lax.rsqrt(_seg_mean_sq(k, bsum_ref) + EPS)) * kg_ref[...]
    kt_ref[0] = k.T.astype(BF16)
    _store_values_with_ones(proj[:, KV_WIDTH:], v_ref)


def _ctx_kv(ctx, mod0, g_mix0, w_kv_b, kg, bsum, ctx_row):
    b, n, d = ctx.shape
    return pl.pallas_call(
        _ctx_kernel,
        grid=(b,),
        in_specs=[
            pl.BlockSpec((1, n, d), lambda bi: (bi, 0, 0)),
            pl.BlockSpec((1, 6, d), lambda bi: (ctx_row, 0, 0)),
            _const_spec((1, d)),
            _const_spec((d, 2 * KV_WIDTH)),
            _const_spec((1, LANES)),
            _const_spec((2 * LANES, LANES)),
        ],
        out_specs=[
            pl.BlockSpec((1, KV_WIDTH, n), lambda bi: (bi, 0, 0)),
            pl.BlockSpec((1, n, 2 * LANES), lambda bi: (bi, 0, 0)),
        ],
        out_shape=[
            jax.ShapeDtypeStruct((b, KV_WIDTH, n), BF16),
            jax.ShapeDtypeStruct((b, n, 2 * LANES), BF16),
        ],
        compiler_params=_params(1),
        name="ctx_kv",
    )(ctx, mod0, g_mix0, w_kv_b, kg, bsum)


def _attn_kernel(q_ref, kt_ref, ve_ref, kct_ref, vce_ref, qg_ref, kg_ref, o_ref, qz_ref, r_ref):
    tq = q_ref.shape[1]
    n_heads = GQA_GROUP * N_KV_HEADS
    left = lax.broadcasted_iota(jnp.int32, (tq, LANES), 1) < HEAD_DIM
    m = (Q_SCALE * HEAD_DIM) * jnp.max(jnp.abs(qg_ref[...])) * jnp.max(jnp.abs(kg_ref[...]))

    for g in range(GQA_GROUP):
        qc = q_ref[0, :, g * LANES:(g + 1) * LANES].astype(F32)
        qz_ref[N_KV_HEADS * g] = jnp.where(left, qc, 0.0).astype(BF16)
        qz_ref[N_KV_HEADS * g + 1] = jnp.where(left, 0.0, qc).astype(BF16)

    def finish(r, u):
        r_ref[u] = r[:, :LANES] / r[:, LANES:]

    def key_tiles():
        for k_ref, v_ref in ((kt_ref, ve_ref), (kct_ref, vce_ref)):
            for j in range(k_ref.shape[2] // KEY_TILE):
                yield (k_ref.at[0, :, j * KEY_TILE:(j + 1) * KEY_TILE],
                       v_ref.at[0, j * KEY_TILE:(j + 1) * KEY_TILE, :])

    def streamed():
        for u in range(n_heads):
            qz = qz_ref[u]
            r = None
            for k_tile, v_tile in key_tiles():
                s = jnp.dot(qz, k_tile[...], preferred_element_type=F32)
                p = jnp.exp2(s - m).astype(BF16)
                d = jnp.dot(p, v_tile[...], preferred_element_type=F32)
                r = d if r is None else r + d
            finish(r, u)

    def exact_max():
        def head(u, carry):
            s1 = jnp.dot(qz_ref[u], kt_ref[0], preferred_element_type=F32)
            s2 = jnp.dot(qz_ref[u], kct_ref[0], preferred_element_type=F32)
            mx = jnp.maximum(jnp.max(s1, axis=-1, keepdims=True), jnp.max(s2, axis=-1, keepdims=True))
            finish(jnp.dot(jnp.exp2(s1 - mx).astype(BF16), ve_ref[0], preferred_element_type=F32)
                   + jnp.dot(jnp.exp2(s2 - mx).astype(BF16), vce_ref[0], preferred_element_type=F32), u)
            return carry
        lax.fori_loop(0, n_heads, head, 0)

    lax.cond(m <= MAX_SAFE_SHIFT, streamed, exact_max)

    for g in range(GQA_GROUP):
        o_ref[0, :, g * LANES:(g + 1) * LANES] = jnp.where(
            left, r_ref[N_KV_HEADS * g], r_ref[N_KV_HEADS * g + 1]).astype(BF16)


def _attention(q, kt, ve, kct, vce, qg, kg, tq):
    b, s, _ = q.shape
    nc = kct.shape[2]
    n_heads = GQA_GROUP * N_KV_HEADS
    assert s % KEY_TILE == 0 and nc % KEY_TILE == 0
    return pl.pallas_call(
        _attn_kernel,
        grid=(b, s // tq),
        in_specs=[
            pl.BlockSpec((1, tq, ATTN_WIDTH), lambda bi, i: (bi, i, 0)),
            pl.BlockSpec((1, KV_WIDTH, s), lambda bi, i: (bi, 0, 0)),
            pl.BlockSpec((1, s, 2 * LANES), lambda bi, i: (bi, 0, 0)),
            pl.BlockSpec((1, KV_WIDTH, nc), lambda bi, i: (bi, 0, 0)),
            pl.BlockSpec((1, nc, 2 * LANES), lambda bi, i: (bi, 0, 0)),
            _const_spec((1, LANES)),
            _const_spec((1, LANES)),
        ],
        out_specs=pl.BlockSpec((1, tq, ATTN_WIDTH), lambda bi, i: (bi, i, 0)),
        out_shape=jax.ShapeDtypeStruct((b, s, ATTN_WIDTH), BF16),
        scratch_shapes=[
            pltpu.VMEM((n_heads, tq, LANES), BF16),
            pltpu.VMEM((n_heads, tq, LANES), F32),
        ],
        compiler_params=_params(2),
        name="attention",
    )(q, kt, ve, kct, vce, qg, kg)


def _post0_kernel(x_ref, a_ref, gm_ref, mod_ref, gffn_ref, woa_ref, wog_ref, w1_ref, w3_ref, w2_ref,
                  o_ref):
    m = mod_ref[0]
    mix = (jnp.dot(a_ref[0], woa_ref[...], preferred_element_type=F32)
           + jnp.dot(gm_ref[0], wog_ref[...], preferred_element_type=F32))
    x1 = x_ref[0] + m[2:3] * mix
    h = _rms_rows(x1, gffn_ref[...]) * (1.0 + m[4:5]) + m[3:4]
    o_ref[0] = x1 + m[5:6] * _swiglu(h.astype(BF16), w1_ref, w3_ref, w2_ref)


def _post0(x, attn, gm, mod0, g_ffn0, woa, wog, w1, w3, w2, tm):
    b, s, d = x.shape
    dff = w1.shape[1]
    return pl.pallas_call(
        _post0_kernel,
        grid=(b, s // tm),
        in_specs=[
            pl.BlockSpec((1, tm, d), lambda bi, i: (bi, i, 0)),
            pl.BlockSpec((1, tm, ATTN_WIDTH), lambda bi, i: (bi, i, 0)),
            pl.BlockSpec((1, tm, GMLP_WIDTH), lambda bi, i: (bi, i, 0)),
            pl.BlockSpec((1, 6, d), lambda bi, i: (bi, 0, 0)),
            _const_spec((1, d)),
            _const_spec((ATTN_WIDTH, d)),
            _const_spec((GMLP_WIDTH, d)),
            _const_spec((d, dff)),
            _const_spec((d, dff)),
            _const_spec((dff, d)),
        ],
        out_specs=pl.BlockSpec((1, tm, d), lambda bi, i: (bi, i, 0)),
        out_shape=jax.ShapeDtypeStruct((b, s, d), F32),
        compiler_params=_params(2),
        name="post0",
    )(x, attn, gm, mod0, g_ffn0, woa, wog, w1, w3, w2)


def _layer1_kernel(x_ref, prev_ref, next_ref, mod_ref, gmix_ref, gffn_ref, ps_ref, band_ref, wp_ref,
                   w1_ref, w3_ref, w2_ref, gfin_ref, o_ref, *, seq_len):
    tm = x_ref.shape[1]
    i = pl.program_id(1)
    m = mod_ref[0]

    def norm_mod(t):
        return _rms_rows(t, gmix_ref[...]) * (1.0 + m[1:2]) + m[0:1]

    x = x_ref[0]
    xn = norm_mod(x)
    xp = jnp.where(i > 0, norm_mod(prev_ref[0]), 0.0)
    xq = jnp.where(i < pl.num_programs(1) - 1, norm_mod(next_ref[0]), 0.0)
    ext = jnp.concatenate([xp, xn, xq], axis=0)
    ext_hi = ext.astype(BF16)
    ext_lo = (ext - ext_hi.astype(F32)).astype(BF16)

    sub = band_ref.shape[1]
    pos = i * tm + lax.broadcasted_iota(jnp.int32, (tm, POOL_GROUP_DIM), 0)
    ys = []
    for gi, w in enumerate(POOL_WINDOWS):
        left_w = w // 2
        right_w = w - 1 - left_w
        cnt = (jnp.minimum(pos + right_w + 1, seq_len) - jnp.maximum(pos - left_w, 0)).astype(F32)
        sl = slice(gi * POOL_GROUP_DIM, (gi + 1) * POOL_GROUP_DIM)
        sums = []
        for r in range(tm // sub):
            rows = slice(r * sub, r * sub + sub + 2 * HALO)
            sums.append(jnp.dot(band_ref[gi], ext_hi[rows, sl], preferred_element_type=F32)
                        + jnp.dot(band_ref[gi], ext_lo[rows, sl], preferred_element_type=F32))
        pooled = jnp.concatenate(sums, axis=0) / cnt - xn[:, sl]
        ys.append(jnp.dot(pooled.astype(BF16), wp_ref[gi], preferred_element_type=F32))
    y = jnp.concatenate(ys, axis=1) * ps_ref[...]
    x1 = x + m[2:3] * y
    h = _rms_rows(x1, gffn_ref[...]) * (1.0 + m[4:5]) + m[3:4]
    x2 = x1 + m[5:6] * _swiglu(h.astype(BF16), w1_ref, w3_ref, w2_ref)
    o_ref[0] = _rms_rows(x2, gfin_ref[...])


def _layer1(x, mod1, g_mix1, g_ffn1, pool_scale, band, wp, w1, w3, w2, g_final, tm):
    b, s, d = x.shape
    dff = w1.shape[1]
    per = tm // HALO
    last = s // HALO - 1
    return pl.pallas_call(
        functools.partial(_layer1_kernel, seq_len=s),
        grid=(b, s // tm),
        in_specs=[
            pl.BlockSpec((1, tm, d), lambda bi, i: (bi, i, 0)),
            pl.BlockSpec((1, HALO, d), lambda bi, i: (bi, jnp.maximum(i * per - 1, 0), 0)),
            pl.BlockSpec((1, HALO, d), lambda bi, i: (bi, jnp.minimum((i + 1) * per, last), 0)),
            pl.BlockSpec((1, 6, d), lambda bi, i: (bi, 0, 0)),
            _const_spec((1, d)),
            _const_spec((1, d)),
            _const_spec((1, d)),
            _const_spec(band.shape),
            _const_spec(wp.shape),
            _const_spec((d, dff)),
            _const_spec((d, dff)),
            _const_spec((dff, d)),
            _const_spec((1, d)),
        ],
        out_specs=pl.BlockSpec((1, tm, d), lambda bi, i: (bi, i, 0)),
        out_shape=jax.ShapeDtypeStruct((b, s, d), F32),
        compiler_params=_params(2),
        name="layer1",
    )(x, x, x, mod1, g_mix1, g_ffn1, pool_scale, band, wp, w1, w3, w2, g_final)


def _rope_tables(n):
    rows = n // GRID_W
    row = jnp.repeat(jnp.arange(rows), GRID_W).astype(F32)
    col = jnp.tile(jnp.arange(GRID_W), rows).astype(F32)
    half = HEAD_DIM // 2
    freqs = ROPE_THETA ** (-jnp.arange(0, half, 2, dtype=F32) / half)
    ang = jnp.concatenate([row[:, None] * freqs, col[:, None] * freqs], axis=-1)
    cos = jnp.tile(jnp.repeat(jnp.cos(ang), 2, axis=1), (1, LANES // HEAD_DIM))
    sin = jnp.tile(jnp.repeat(jnp.sin(ang), 2, axis=1), (1, LANES // HEAD_DIM))
    sign = jnp.where(jnp.arange(LANES) % 2 == 0, -1.0, 1.0).astype(F32)
    return cos, sin * sign


def _band_matrices(sub):
    t = np.arange(sub)[:, None]
    e = np.arange(sub + 2 * HALO)[None, :]
    mats = []
    for w in POOL_WINDOWS:
        left = w // 2
        right = w - 1 - left
        mats.append(((e >= t + HALO - left) & (e <= t + HALO + right)).astype(np.float32))
    return jnp.asarray(np.stack(mats), dtype=BF16)


def kernel(x, c, ctx, c_ctx, w_ada, b_ada, g_mix, g_ffn, w_in, w_out, q_norm, k_norm, gmlp_norm,
           w_spatial, b_spatial, w_pool, pool_scale, w1, w3, w2, g_final):
    b, s, d = x.shape
    depth = w_ada.shape[0]
    assert depth == 2 and d == D_MODEL and s % CHUNK == 0
    tm_pre, tq, tm_ffn = 512, 512, 512

    cond = jnp.concatenate([c, c_ctx[None], jnp.zeros((COND_ROWS - b - 1, d), F32)], axis=0)
    mods = _adaln(cond, w_ada, b_ada).reshape(depth, COND_ROWS, 6, d)

    wi = w_in[0]
    wq = wi[:, :ATTN_WIDTH].reshape(d, N_KV_HEADS, GQA_GROUP, HEAD_DIM).transpose(0, 2, 1, 3)
    w_in_b = jnp.concatenate([wq.reshape(d, ATTN_WIDTH), wi[:, ATTN_WIDTH:]], axis=1).astype(BF16)
    w_kv_b = wi[:, ATTN_WIDTH:ATTN_WIDTH + 2 * KV_WIDTH].astype(BF16)
    wo = w_out[0]
    woa = wo[:ATTN_WIDTH].reshape(N_KV_HEADS, GQA_GROUP, HEAD_DIM, d).transpose(1, 0, 2, 3)
    woa = woa.reshape(ATTN_WIDTH, d).astype(BF16)
    wog = wo[ATTN_WIDTH:].astype(BF16)
    cos_t, sin_t = _rope_tables(s)
    qg = jnp.tile(q_norm[0], LANES // HEAD_DIM)[None]
    kg = jnp.tile(k_norm[0], LANES // HEAD_DIM)[None]
    gv = gmlp_norm[0].reshape(1, GMLP_WIDTH)
    seg = np.arange(LANES) // HEAD_DIM
    bsum = jnp.asarray(np.tile((seg[:, None] == seg[None, :]).astype(np.float32), (2, 1)), dtype=BF16)
    ws = w_spatial[0]
    ws2 = jnp.concatenate([ws[0::2], ws[1::2]], axis=2).astype(BF16)
    bs_t = jnp.repeat(b_spatial[0].T, GMLP_GROUP_DIM, axis=1)
    w1b, w3b, w2b = w1.astype(BF16), w3.astype(BF16), w2.astype(BF16)

    q, kt, v, gm = _pre0(x, mods[0], g_mix[0][None], w_in_b, cos_t, sin_t, qg, kg, gv, bsum, ws2,
                         bs_t, tm_pre)
    kct, vc = _ctx_kv(ctx, mods[0], g_mix[0][None], w_kv_b, kg, bsum, b)
    attn = _attention(q, kt, v, kct, vc, qg, kg, tq)
    x1 = _post0(x, attn, gm, mods[0], g_ffn[0][None], woa, wog, w1b[0], w3b[0], w2b[0], tm_ffn)

    band = _band_matrices(CHUNK)
    return _layer1(x1, mods[1], g_mix[1][None], g_ffn[1][None], pool_scale[0][None], band,
                   w_pool[0].astype(BF16), w1b[1], w3b[1], w2b[1], g_final[None], tm_ffn)
```

```python
import functools

import numpy as np
import jax
import jax.numpy as jnp
from jax import lax
from jax.experimental import pallas as pl
from jax.experimental.pallas import tpu as pltpu

D_MODEL = 1024
GRID_W = 64
N_HEADS = 8
N_KV_HEADS = 2
HEAD_DIM = 64
GQA_GROUP = N_HEADS // N_KV_HEADS
ATTN_WIDTH = N_HEADS * HEAD_DIM
KV_WIDTH = N_KV_HEADS * HEAD_DIM
ROPE_THETA = 10000.0
GMLP_GROUPS = 8
GMLP_GROUP_DIM = 64
GMLP_WIDTH = GMLP_GROUPS * GMLP_GROUP_DIM
CHUNK = 128
POOL_WINDOWS = (2, 4, 8, 16)
POOL_GROUP_DIM = D_MODEL // len(POOL_WINDOWS)
EPS = 1e-6
Q_SCALE = float(HEAD_DIM ** -0.5 * np.log2(np.e))

LANES = 128
HALO = 8
COND_ROWS = 16
VMEM_LIMIT = 56 * 1024 * 1024
KEY_TILE = 256
MAX_SAFE_SHIFT = 60.0

F32 = jnp.float32
BF16 = jnp.bfloat16


def _const_spec(shape):
    nd = len(shape)
    return pl.BlockSpec(shape, lambda *_: (0,) * nd, pipeline_mode=pl.Buffered(1))


def _layer_spec(shape, layer):
    nd = len(shape)
    return pl.BlockSpec((None,) + tuple(shape), lambda *_: (layer,) + (0,) * nd,
                        pipeline_mode=pl.Buffered(1))


def _params(n_axes):
    return pltpu.CompilerParams(dimension_semantics=("arbitrary",) * n_axes,
                                vmem_limit_bytes=VMEM_LIMIT)


def _rms_rows(x, gain):
    ms = jnp.mean(x * x, axis=-1, keepdims=True)
    return (x * lax.rsqrt(ms + EPS)) * gain


def _seg_mean_sq(t, bsum_ref):
    sq = t * t
    hi = sq.astype(BF16)
    lo = (sq - hi.astype(F32)).astype(BF16)
    ss = jnp.dot(jnp.concatenate([hi, lo], axis=1), bsum_ref[...], preferred_element_type=F32)
    return ss * (1.0 / HEAD_DIM)


def _swiglu(h, w1_ref, w3_ref, w2_ref):
    a = jnp.dot(h, w1_ref[...], preferred_element_type=F32)
    b = jnp.dot(h, w3_ref[...], preferred_element_type=F32)
    g = (a * jax.nn.sigmoid(a) * b).astype(BF16)
    return jnp.dot(g, w2_ref[...], preferred_element_type=F32)


def _adaln_kernel(cond_ref, w_ref, b_ref, o_ref):
    s = cond_ref[...]
    s = (s * jax.nn.sigmoid(s)).astype(BF16)
    o_ref[0] = jnp.dot(s, w_ref[0].astype(BF16), preferred_element_type=F32) + b_ref[0]


def _adaln(cond, w_ada, b_ada, tn=1536):
    depth, d, n = w_ada.shape
    return pl.pallas_call(
        _adaln_kernel,
        grid=(depth, n // tn),
        in_specs=[
            pl.BlockSpec((COND_ROWS, d), lambda l, j: (0, 0)),
            pl.BlockSpec((1, d, tn), lambda l, j: (l, 0, j)),
            pl.BlockSpec((1, 1, tn), lambda l, j: (l, 0, j)),
        ],
        out_specs=pl.BlockSpec((1, COND_ROWS, tn), lambda l, j: (l, 0, j)),
        out_shape=jax.ShapeDtypeStruct((depth, COND_ROWS, n), F32),
        compiler_params=_params(2),
        name="adaln",
    )(cond, w_ada, b_ada.reshape(depth, 1, n))


def _rope(t, cos, sin_signed, even_lane):
    partner = jnp.where(even_lane, pltpu.roll(t, LANES - 1, 1), pltpu.roll(t, 1, 1))
    return t * cos + partner * sin_signed


def _gelu(x):
    return 0.5 * x * (1.0 + lax.erf(x * np.float32(np.sqrt(0.5))))


def _store_values_with_ones(v, ve_ref):
    ve_ref[0, :, :LANES] = v.astype(BF16)
    ve_ref[0, :, LANES:] = jnp.ones(v.shape, BF16)


def _pre0_kernel(x_ref, mod_ref, gmix_ref, win_ref, cos_ref, sin_ref, qg_ref, kg_ref, gv_ref,
                 bsum_ref, ws_ref, bs_ref, q_ref, kt_ref, v_ref, gm_ref):
    tm = x_ref.shape[1]
    m = mod_ref[0]
    h = _rms_rows(x_ref[0], gmix_ref[...]) * (1.0 + m[1:2]) + m[0:1]
    proj = jnp.dot(h.astype(BF16), win_ref[...], preferred_element_type=F32)

    lane = lax.broadcasted_iota(jnp.int32, (tm, LANES), 1)
    even_lane = (lane % 2) == 0
    cos = cos_ref[...]
    sin_signed = sin_ref[...]

    def head_norm_rope(t, gain):
        tn = (t * lax.rsqrt(_seg_mean_sq(t, bsum_ref) + EPS)) * gain
        return _rope(tn, cos, sin_signed, even_lane)

    for g in range(GQA_GROUP):
        t = proj[:, g * LANES:(g + 1) * LANES]
        q_ref[0, :, g * LANES:(g + 1) * LANES] = (head_norm_rope(t, qg_ref[...]) * Q_SCALE).astype(BF16)
    k = head_norm_rope(proj[:, ATTN_WIDTH:ATTN_WIDTH + KV_WIDTH], kg_ref[...])
    kt_ref[0] = k.T.astype(BF16)
    _store_values_with_ones(proj[:, ATTN_WIDTH + KV_WIDTH:ATTN_WIDTH + 2 * KV_WIDTH], v_ref)

    u0 = ATTN_WIDTH + 2 * KV_WIDTH
    g0 = u0 + GMLP_WIDTH
    left = lax.broadcasted_iota(jnp.int32, (CHUNK, LANES), 1) < GMLP_GROUP_DIM
    for j in range(GMLP_WIDTH // LANES):
        u = _gelu(proj[:, u0 + j * LANES:u0 + (j + 1) * LANES])
        vv = _gelu(proj[:, g0 + j * LANES:g0 + (j + 1) * LANES])
        vg = (vv * lax.rsqrt(_seg_mean_sq(vv, bsum_ref) + EPS)) * gv_ref[:, j * LANES:(j + 1) * LANES]
        bias = bs_ref[:, j * LANES:(j + 1) * LANES]
        for n in range(tm // CHUNK):
            blk = vg[n * CHUNK:(n + 1) * CHUNK]
            rhs = jnp.concatenate([jnp.where(left, blk, 0.0), jnp.where(left, 0.0, blk)],
                                  axis=0).astype(BF16)
            mixed = jnp.dot(ws_ref[j], rhs, preferred_element_type=F32) + bias
            gm_ref[0, n * CHUNK:(n + 1) * CHUNK, j * LANES:(j + 1) * LANES] = (
                u[n * CHUNK:(n + 1) * CHUNK] * mixed).astype(BF16)


def _pre0(x, mod0, g_mix0, w_in_b, cos_t, sin_t, qg, kg, gv, bsum, ws2, bs_t, tm):
    b, s, d = x.shape
    nw = w_in_b.shape[1]
    return pl.pallas_call(
        _pre0_kernel,
        grid=(b, s // tm),
        in_specs=[
            pl.BlockSpec((1, tm, d), lambda bi, i: (bi, i, 0)),
            pl.BlockSpec((1, 6, d), lambda bi, i: (bi, 0, 0)),
            _const_spec((1, d)),
            _const_spec((d, nw)),
            pl.BlockSpec((tm, LANES), lambda bi, i: (i, 0)),
            pl.BlockSpec((tm, LANES), lambda bi, i: (i, 0)),
            _const_spec((1, LANES)),
            _const_spec((1, LANES)),
            _const_spec((1, GMLP_WIDTH)),
            _const_spec((2 * LANES, LANES)),
            _const_spec(ws2.shape),
            _const_spec(bs_t.shape),
        ],
        out_specs=[
            pl.BlockSpec((1, tm, ATTN_WIDTH), lambda bi, i: (bi, i, 0)),
            pl.BlockSpec((1, KV_WIDTH, tm), lambda bi, i: (bi, 0, i)),
            pl.BlockSpec((1, tm, 2 * LANES), lambda bi, i: (bi, i, 0)),
            pl.BlockSpec((1, tm, GMLP_WIDTH), lambda bi, i: (bi, i, 0)),
        ],
        out_shape=[
            jax.ShapeDtypeStruct((b, s, ATTN_WIDTH), BF16),
            jax.ShapeDtypeStruct((b, KV_WIDTH, s), BF16),
            jax.ShapeDtypeStruct((b, s, 2 * LANES), BF16),
            jax.ShapeDtypeStruct((b, s, GMLP_WIDTH), BF16),
        ],
        compiler_params=_params(2),
        name="pre0",
    )(x, mod0, g_mix0, w_in_b, cos_t, sin_t, qg, kg, gv, bsum, ws2, bs_t)


def _ctx_kernel(c_ref, mod_ref, gmix_ref, wkv_ref, kg_ref, bsum_ref, kt_ref, v_ref):
    m = mod_ref[0]
    h = _rms_rows(c_ref[0], gmix_ref[...]) * (1.0 + m[1:2]) + m[0:1]
    proj = jnp.dot(h.astype(BF16), wkv_ref[...], preferred_element_type=F32)
    k = proj[:, :KV_WIDTH]
    k = (k * lax.rsqrt(_seg_mean_sq(k, bsum_ref) + EPS)) * kg_ref[...]
    kt_ref[0] = k.T.astype(BF16)
    _store_values_with_ones(proj[:, KV_WIDTH:], v_ref)


def _ctx_kv(ctx, mod0, g_mix0, w_kv_b, kg, bsum, ctx_row):
    b, n, d = ctx.shape
    return pl.pallas_call(
        _ctx_kernel,
        grid=(b,),
        in_specs=[
            pl.BlockSpec((1, n, d), lambda bi: (bi, 0, 0)),
            pl.BlockSpec((1, 6, d), lambda bi: (ctx_row, 0, 0)),
            _const_spec((1, d)),
            _const_spec((d, 2 * KV_WIDTH)),
            _const_spec((1, LANES)),
            _const_spec((2 * LANES, LANES)),
        ],
        out_specs=[
            pl.BlockSpec((1, KV_WIDTH, n), lambda bi: (bi, 0, 0)),
            pl.BlockSpec((1, n, 2 * LANES), lambda bi: (bi, 0, 0)),
        ],
        out_shape=[
            jax.ShapeDtypeStruct((b, KV_WIDTH, n), BF16),
            jax.ShapeDtypeStruct((b, n, 2 * LANES), BF16),
        ],
        compiler_params=_params(1),
        name="ctx_kv",
    )(ctx, mod0, g_mix0, w_kv_b, kg, bsum)


def _attn_kernel(q_ref, kt_ref, ve_ref, kct_ref, vce_ref, qg_ref, kg_ref, o_ref, qz_ref, r_ref):
    tq = q_ref.shape[1]
    n_heads = GQA_GROUP * N_KV_HEADS
    left = lax.broadcasted_iota(jnp.int32, (tq, LANES), 1) < HEAD_DIM
    m = (Q_SCALE * HEAD_DIM) * jnp.max(jnp.abs(qg_ref[...])) * jnp.max(jnp.abs(kg_ref[...]))

    for g in range(GQA_GROUP):
        qc = q_ref[0, :, g * LANES:(g + 1) * LANES].astype(F32)
        qz_ref[N_KV_HEADS * g] = jnp.where(left, qc, 0.0).astype(BF16)
        qz_ref[N_KV_HEADS * g + 1] = jnp.where(left, 0.0, qc).astype(BF16)

    def finish(r, u):
        r_ref[u] = r[:, :LANES] / r[:, LANES:]

    def key_tiles():
        for k_ref, v_ref in ((kt_ref, ve_ref), (kct_ref, vce_ref)):
            for j in range(k_ref.shape[2] // KEY_TILE):
                yield (k_ref.at[0, :, j * KEY_TILE:(j + 1) * KEY_TILE],
                       v_ref.at[0, j * KEY_TILE:(j + 1) * KEY_TILE, :])

    def streamed():
        for u in range(n_heads):
            qz = qz_ref[u]
            r = None
            for k_tile, v_tile in key_tiles():
                s = jnp.dot(qz, k_tile[...], preferred_element_type=F32)
                p = jnp.exp2(s - m).astype(BF16)
                d = jnp.dot(p, v_tile[...], preferred_element_type=F32)
                r = d if r is None else r + d
            finish(r, u)

    def exact_max():
        def head(u, carry):
            s1 = jnp.dot(qz_ref[u], kt_ref[0], preferred_element_type=F32)
            s2 = jnp.dot(qz_ref[u], kct_ref[0], preferred_element_type=F32)
            mx = jnp.maximum(jnp.max(s1, axis=-1, keepdims=True), jnp.max(s2, axis=-1, keepdims=True))
            finish(jnp.dot(jnp.exp2(s1 - mx).astype(BF16), ve_ref[0], preferred_element_type=F32)
                   + jnp.dot(jnp.exp2(s2 - mx).astype(BF16), vce_ref[0], preferred_element_type=F32), u)
            return carry
        lax.fori_loop(0, n_heads, head, 0)

    lax.cond(m <= MAX_SAFE_SHIFT, streamed, exact_max)

    for g in range(GQA_GROUP):
        o_ref[0, :, g * LANES:(g + 1) * LANES] = jnp.where(
            left, r_ref[N_KV_HEADS * g], r_ref[N_KV_HEADS * g + 1]).astype(BF16)


def _attention(q, kt, ve, kct, vce, qg, kg, tq):
    b, s, _ = q.shape
    nc = kct.shape[2]
    n_heads = GQA_GROUP * N_KV_HEADS
    assert s % KEY_TILE == 0 and nc % KEY_TILE == 0
    return pl.pallas_call(
        _attn_kernel,
        grid=(b, s // tq),
        in_specs=[
            pl.BlockSpec((1, tq, ATTN_WIDTH), lambda bi, i: (bi, i, 0)),
            pl.BlockSpec((1, KV_WIDTH, s), lambda bi, i: (bi, 0, 0)),
            pl.BlockSpec((1, s, 2 * LANES), lambda bi, i: (bi, 0, 0)),
            pl.BlockSpec((1, KV_WIDTH, nc), lambda bi, i: (bi, 0, 0)),
            pl.BlockSpec((1, nc, 2 * LANES), lambda bi, i: (bi, 0, 0)),
            _const_spec((1, LANES)),
            _const_spec((1, LANES)),
        ],
        out_specs=pl.BlockSpec((1, tq, ATTN_WIDTH), lambda bi, i: (bi, i, 0)),
        out_shape=jax.ShapeDtypeStruct((b, s, ATTN_WIDTH), BF16),
        scratch_shapes=[
            pltpu.VMEM((n_heads, tq, LANES), BF16),
            pltpu.VMEM((n_heads, tq, LANES), F32),
        ],
        compiler_params=_params(2),
        name="attention",
    )(q, kt, ve, kct, vce, qg, kg)


def _post0_kernel(x_ref, a_ref, gm_ref, mod_ref, gffn_ref, woa_ref, wog_ref, w1_ref, w3_ref, w2_ref,
                  o_ref, h0_ref, h1_ref, r0_ref, r1_ref, g0_ref, g1_ref):
    t = pl.program_id(0)
    slots = ((h0_ref, r0_ref, g0_ref), (h1_ref, r1_ref, g1_ref))

    def mix_stage(slot):
        h_ref, res_ref, gate_ref = slots[slot]
        m = mod_ref[0]
        mix = (jnp.dot(a_ref[0], woa_ref[...], preferred_element_type=F32)
               + jnp.dot(gm_ref[0], wog_ref[...], preferred_element_type=F32))
        x1 = x_ref[0] + m[2:3] * mix
        res_ref[...] = x1
        h_ref[...] = (_rms_rows(x1, gffn_ref[...]) * (1.0 + m[4:5]) + m[3:4]).astype(BF16)
        gate_ref[...] = jnp.broadcast_to(m[5:6], gate_ref.shape)

    def ffn_stage(slot):
        h_ref, res_ref, gate_ref = slots[slot]
        o_ref[0] = res_ref[...] + gate_ref[0:1, :] * _swiglu(h_ref[...], w1_ref, w3_ref, w2_ref)

    @pl.when(t == 0)
    def _():
        mix_stage(0)

    @pl.when(jnp.logical_and(t > 0, t % 2 == 0))
    def _():
        mix_stage(0)
        ffn_stage(1)

    @pl.when(t % 2 == 1)
    def _():
        mix_stage(1)
        ffn_stage(0)


def _post0(x, attn, gm, mod0, g_ffn0, woa, wog, w1, w3, w2, layer, tm):
    b, s, d = x.shape
    dff = w1.shape[2]
    per_b = s // tm
    n_tiles = b * per_b
    assert n_tiles % 2 == 0

    def cur(t):
        tile = jnp.minimum(t, n_tiles - 1)
        return tile // per_b, tile % per_b

    def prev(t):
        tile = jnp.maximum(t - 1, 0)
        return tile // per_b, tile % per_b

    return pl.pallas_call(
        _post0_kernel,
        grid=(n_tiles + 1,),
        in_specs=[
            pl.BlockSpec((1, tm, d), lambda t: (*cur(t), 0)),
            pl.BlockSpec((1, tm, ATTN_WIDTH), lambda t: (*cur(t), 0)),
            pl.BlockSpec((1, tm, GMLP_WIDTH), lambda t: (*cur(t), 0)),
            pl.BlockSpec((1, 6, d), lambda t: (cur(t)[0], 0, 0)),
            _const_spec((1, d)),
            _const_spec((ATTN_WIDTH, d)),
            _const_spec((GMLP_WIDTH, d)),
            _layer_spec((d, dff), layer),
            _layer_spec((d, dff), layer),
            _layer_spec((dff, d), layer),
        ],
        out_specs=pl.BlockSpec((1, tm, d), lambda t: (*prev(t), 0)),
        out_shape=jax.ShapeDtypeStruct((b, s, d), F32),
        scratch_shapes=[
            pltpu.VMEM((tm, d), BF16), pltpu.VMEM((tm, d), BF16),
            pltpu.VMEM((tm, d), F32), pltpu.VMEM((tm, d), F32),
            pltpu.VMEM((8, d), F32), pltpu.VMEM((8, d), F32),
        ],
        compiler_params=_params(1),
        name="post0",
    )(x, attn, gm, mod0, g_ffn0, woa, wog, w1, w3, w2)


def _layer1_kernel(x_ref, prev_ref, next_ref, mod_ref, gmix_ref, gffn_ref, ps_ref, band_ref, wp_ref,
                   w1_ref, w3_ref, w2_ref, gfin_ref, o_ref, h0_ref, h1_ref, r0_ref, r1_ref, g0_ref,
                   g1_ref, *, seq_len, tiles_per_seq, n_tiles):
    tm = x_ref.shape[1]
    t = pl.program_id(0)
    i = jnp.minimum(t, n_tiles - 1) % tiles_per_seq
    slots = ((h0_ref, r0_ref, g0_ref), (h1_ref, r1_ref, g1_ref))
    sub = band_ref.shape[1]

    def mixer_parts(slot):
        h_ref, res_ref, gate_ref = slots[slot]
        m = mod_ref[0]
        st = {}

        def norm_mod(v):
            return _rms_rows(v, gmix_ref[...]) * (1.0 + m[1:2]) + m[0:1]

        def window_sums():
            x = x_ref[0]
            xn = norm_mod(x)
            xp = jnp.where(i > 0, norm_mod(prev_ref[0]), 0.0)
            xq = jnp.where(i < tiles_per_seq - 1, norm_mod(next_ref[0]), 0.0)
            ext = jnp.concatenate([xp, xn, xq], axis=0)
            ext_hi = ext.astype(BF16)
            ext_lo = (ext - ext_hi.astype(F32)).astype(BF16)
            sums = []
            for gi in range(len(POOL_WINDOWS)):
                sl = slice(gi * POOL_GROUP_DIM, (gi + 1) * POOL_GROUP_DIM)
                parts = []
                for r in range(tm // sub):
                    rows = slice(r * sub, r * sub + sub + 2 * HALO)
                    parts.append(jnp.dot(band_ref[gi], ext_hi[rows, sl], preferred_element_type=F32)
                                 + jnp.dot(band_ref[gi], ext_lo[rows, sl], preferred_element_type=F32))
                sums.append(jnp.concatenate(parts, axis=0))
            st.update(x=x, xn=xn, sums=sums)

        def group_mix():
            pos = i * tm + lax.broadcasted_iota(jnp.int32, (tm, POOL_GROUP_DIM), 0)
            ys = []
            for gi, w in enumerate(POOL_WINDOWS):
                left_w = w // 2
                right_w = w - 1 - left_w
                cnt = (jnp.minimum(pos + right_w + 1, seq_len) - jnp.maximum(pos - left_w, 0)).astype(F32)
                sl = slice(gi * POOL_GROUP_DIM, (gi + 1) * POOL_GROUP_DIM)
                pooled = st["sums"][gi] / cnt - st["xn"][:, sl]
                ys.append(jnp.dot(pooled.astype(BF16), wp_ref[gi], preferred_element_type=F32))
            st["y"] = jnp.concatenate(ys, axis=1)

        def hand_over():
            x1 = st["x"] + m[2:3] * (st["y"] * ps_ref[...])
            res_ref[...] = x1
            h_ref[...] = (_rms_rows(x1, gffn_ref[...]) * (1.0 + m[4:5]) + m[3:4]).astype(BF16)
            gate_ref[...] = jnp.broadcast_to(m[5:6], gate_ref.shape)

        return window_sums, group_mix, hand_over

    def step(new_slot, old_slot):
        window_sums, group_mix, hand_over = mixer_parts(new_slot)
        h_ref, res_ref, gate_ref = slots[old_slot]
        h = h_ref[...]
        a = jnp.dot(h, w1_ref[...], preferred_element_type=F32)
        window_sums()
        b = jnp.dot(h, w3_ref[...], preferred_element_type=F32)
        group_mix()
        f = jnp.dot((a * jax.nn.sigmoid(a) * b).astype(BF16), w2_ref[...], preferred_element_type=F32)
        hand_over()
        o_ref[0] = _rms_rows(res_ref[...] + gate_ref[0:1, :] * f, gfin_ref[...])

    @pl.when(t == 0)
    def _():
        for part in mixer_parts(0):
            part()

    @pl.when(jnp.logical_and(t > 0, t % 2 == 0))
    def _():
        step(0, 1)

    @pl.when(t % 2 == 1)
    def _():
        step(1, 0)


def _layer1(x, mod1, g_mix1, g_ffn1, pool_scale, band, wp, w1, w3, w2, g_final, layer, tm):
    b, s, d = x.shape
    dff = w1.shape[2]
    per_b = s // tm
    n_tiles = b * per_b
    assert n_tiles % 2 == 0
    per = tm // HALO
    last = s // HALO - 1

    def cur(t):
        tile = jnp.minimum(t, n_tiles - 1)
        return tile // per_b, tile % per_b

    def prev(t):
        tile = jnp.maximum(t - 1, 0)
        return tile // per_b, tile % per_b

    def halo_before(t):
        bi, i = cur(t)
        return bi, jnp.maximum(i * per - 1, 0), 0

    def halo_after(t):
        bi, i = cur(t)
        return bi, jnp.minimum((i + 1) * per, last), 0

    return pl.pallas_call(
        functools.partial(_layer1_kernel, seq_len=s, tiles_per_seq=per_b, n_tiles=n_tiles),
        grid=(n_tiles + 1,),
        in_specs=[
            pl.BlockSpec((1, tm, d), lambda t: (*cur(t), 0)),
            pl.BlockSpec((1, HALO, d), halo_before),
            pl.BlockSpec((1, HALO, d), halo_after),
            pl.BlockSpec((1, 6, d), lambda t: (cur(t)[0], 0, 0)),
            _const_spec((1, d)),
            _const_spec((1, d)),
            _const_spec((1, d)),
            _const_spec(band.shape),
            _const_spec(wp.shape),
            _layer_spec((d, dff), layer),
            _layer_spec((d, dff), layer),
            _layer_spec((dff, d), layer),
            _const_spec((1, d)),
        ],
        out_specs=pl.BlockSpec((1, tm, d), lambda t: (*prev(t), 0)),
        out_shape=jax.ShapeDtypeStruct((b, s, d), F32),
        scratch_shapes=[
            pltpu.VMEM((tm, d), BF16), pltpu.VMEM((tm, d), BF16),
            pltpu.VMEM((tm, d), F32), pltpu.VMEM((tm, d), F32),
            pltpu.VMEM((8, d), F32), pltpu.VMEM((8, d), F32),
        ],
        compiler_params=_params(1),
        name="layer1",
    )(x, x, x, mod1, g_mix1, g_ffn1, pool_scale, band, wp, w1, w3, w2, g_final)


def _rope_tables(n):
    rows = n // GRID_W
    row = jnp.repeat(jnp.arange(rows), GRID_W).astype(F32)
    col = jnp.tile(jnp.arange(GRID_W), rows).astype(F32)
    half = HEAD_DIM // 2
    freqs = ROPE_THETA ** (-jnp.arange(0, half, 2, dtype=F32) / half)
    ang = jnp.concatenate([row[:, None] * freqs, col[:, None] * freqs], axis=-1)
    cos = jnp.tile(jnp.repeat(jnp.cos(ang), 2, axis=1), (1, LANES // HEAD_DIM))
    sin = jnp.tile(jnp.repeat(jnp.sin(ang), 2, axis=1), (1, LANES // HEAD_DIM))
    sign = jnp.where(jnp.arange(LANES) % 2 == 0, -1.0, 1.0).astype(F32)
    return cos, sin * sign


def _band_matrices(sub):
    t = np.arange(sub)[:, None]
    e = np.arange(sub + 2 * HALO)[None, :]
    mats = []
    for w in POOL_WINDOWS:
        left = w // 2
        right = w - 1 - left
        mats.append(((e >= t + HALO - left) & (e <= t + HALO + right)).astype(np.float32))
    return jnp.asarray(np.stack(mats), dtype=BF16)


def kernel(x, c, ctx, c_ctx, w_ada, b_ada, g_mix, g_ffn, w_in, w_out, q_norm, k_norm, gmlp_norm,
           w_spatial, b_spatial, w_pool, pool_scale, w1, w3, w2, g_final):
    b, s, d = x.shape
    depth = w_ada.shape[0]
    assert depth == 2 and d == D_MODEL and s % CHUNK == 0
    tm_pre, tq, tm_ffn = 512, 512, 512

    cond = jnp.concatenate([c, c_ctx[None], jnp.zeros((COND_ROWS - b - 1, d), F32)], axis=0)
    mods = _adaln(cond, w_ada, b_ada).reshape(depth, COND_ROWS, 6, d)

    wi = w_in[0]
    wq = wi[:, :ATTN_WIDTH].reshape(d, N_KV_HEADS, GQA_GROUP, HEAD_DIM).transpose(0, 2, 1, 3)
    w_in_b = jnp.concatenate([wq.reshape(d, ATTN_WIDTH), wi[:, ATTN_WIDTH:]], axis=1).astype(BF16)
    w_kv_b = wi[:, ATTN_WIDTH:ATTN_WIDTH + 2 * KV_WIDTH].astype(BF16)
    wo = w_out[0]
    woa = wo[:ATTN_WIDTH].reshape(N_KV_HEADS, GQA_GROUP, HEAD_DIM, d).transpose(1, 0, 2, 3)
    woa = woa.reshape(ATTN_WIDTH, d).astype(BF16)
    wog = wo[ATTN_WIDTH:].astype(BF16)
    cos_t, sin_t = _rope_tables(s)
    qg = jnp.tile(q_norm[0], LANES // HEAD_DIM)[None]
    kg = jnp.tile(k_norm[0], LANES // HEAD_DIM)[None]
    gv = gmlp_norm[0].reshape(1, GMLP_WIDTH)
    seg = np.arange(LANES) // HEAD_DIM
    bsum = jnp.asarray(np.tile((seg[:, None] == seg[None, :]).astype(np.float32), (2, 1)), dtype=BF16)
    ws = w_spatial[0]
    ws2 = jnp.concatenate([ws[0::2], ws[1::2]], axis=2).astype(BF16)
    bs_t = jnp.repeat(b_spatial[0].T, GMLP_GROUP_DIM, axis=1)
    w1b, w3b, w2b = w1.astype(BF16), w3.astype(BF16), w2.astype(BF16)

    q, kt, v, gm = _pre0(x, mods[0], g_mix[0][None], w_in_b, cos_t, sin_t, qg, kg, gv, bsum, ws2,
                         bs_t, tm_pre)
    kct, vc = _ctx_kv(ctx, mods[0], g_mix[0][None], w_kv_b, kg, bsum, b)
    attn = _attention(q, kt, v, kct, vc, qg, kg, tq)
    x1 = _post0(x, attn, gm, mods[0], g_ffn[0][None], woa, wog, w1b, w3b, w2b, 0, tm_ffn)

    band = _band_matrices(CHUNK)
    return _layer1(x1, mods[1], g_mix[1][None], g_ffn[1][None], pool_scale[0][None], band,
                   w_pool[0].astype(BF16), w1b, w3b, w2b, g_final[None], 1, tm_ffn)
```

```python
import functools

import numpy as np
import jax
import jax.numpy as jnp
from jax import lax
from jax.experimental import pallas as pl
from jax.experimental.pallas import tpu as pltpu

D_MODEL = 1024
GRID_W = 64
N_HEADS = 8
N_KV_HEADS = 2
HEAD_DIM = 64
GQA_GROUP = N_HEADS // N_KV_HEADS
ATTN_WIDTH = N_HEADS * HEAD_DIM
KV_WIDTH = N_KV_HEADS * HEAD_DIM
ROPE_THETA = 10000.0
GMLP_GROUPS = 8
GMLP_GROUP_DIM = 64
GMLP_WIDTH = GMLP_GROUPS * GMLP_GROUP_DIM
CHUNK = 128
POOL_WINDOWS = (2, 4, 8, 16)
POOL_GROUP_DIM = D_MODEL // len(POOL_WINDOWS)
EPS = 1e-6
Q_SCALE = float(HEAD_DIM ** -0.5 * np.log2(np.e))

LANES = 128
HALO = 8
COND_ROWS = 16
VMEM_LIMIT = 56 * 1024 * 1024
POOL_SUB = 64
KEY_TILE = 256
MAX_SAFE_SHIFT = 60.0

F32 = jnp.float32
BF16 = jnp.bfloat16


def _const_spec(shape):
    nd = len(shape)
    return pl.BlockSpec(shape, lambda *_: (0,) * nd, pipeline_mode=pl.Buffered(1))


def _layer_spec(shape, layer):
    nd = len(shape)
    return pl.BlockSpec((None,) + tuple(shape), lambda *_: (layer,) + (0,) * nd,
                        pipeline_mode=pl.Buffered(1))


def _params(n_axes):
    return pltpu.CompilerParams(dimension_semantics=("arbitrary",) * n_axes,
                                vmem_limit_bytes=VMEM_LIMIT)


def _rms_rows(x, gain):
    ms = jnp.mean(x * x, axis=-1, keepdims=True)
    return (x * lax.rsqrt(ms + EPS)) * gain


def _norm_modulate(x, gain, shift, scale):
    return _rms_rows(x, gain * (1.0 + scale)) + shift


def _seg_mean_sq(ta, tb, bsum_ref):
    sq = jnp.concatenate([(ta * ta).astype(BF16), (tb * tb).astype(BF16)], axis=1)
    ss = jnp.dot(sq, bsum_ref[...], preferred_element_type=F32) * (1.0 / HEAD_DIM)
    return ss[:, :LANES], ss[:, LANES:]


def _swiglu(h, w1_ref, w3_ref, w2_ref):
    a = jnp.dot(h, w1_ref[...], preferred_element_type=F32)
    b = jnp.dot(h, w3_ref[...], preferred_element_type=F32)
    g = (a * jax.nn.sigmoid(a) * b).astype(BF16)
    return jnp.dot(g, w2_ref[...], preferred_element_type=F32)


def _adaln_kernel(cond_ref, w_ref, b_ref, o_ref):
    s = cond_ref[...]
    s = (s * jax.nn.sigmoid(s)).astype(BF16)
    o_ref[0] = jnp.dot(s, w_ref[0].astype(BF16), preferred_element_type=F32) + b_ref[0]


def _adaln(cond, w_ada, b_ada, tn=1536):
    depth, d, n = w_ada.shape
    return pl.pallas_call(
        _adaln_kernel,
        grid=(depth, n // tn),
        in_specs=[
            pl.BlockSpec((COND_ROWS, d), lambda l, j: (0, 0)),
            pl.BlockSpec((1, d, tn), lambda l, j: (l, 0, j)),
            pl.BlockSpec((1, 1, tn), lambda l, j: (l, 0, j)),
        ],
        out_specs=pl.BlockSpec((1, COND_ROWS, tn), lambda l, j: (l, 0, j)),
        out_shape=jax.ShapeDtypeStruct((depth, COND_ROWS, n), F32),
        compiler_params=_params(2),
        name="adaln",
    )(cond, w_ada, b_ada.reshape(depth, 1, n))


def _rope(t, cos, sin_signed, even_lane):
    partner = jnp.where(even_lane, pltpu.roll(t, LANES - 1, 1), pltpu.roll(t, 1, 1))
    return t * cos + partner * sin_signed


def _gelu(x):
    return 0.5 * x * (1.0 + lax.erf(x * np.float32(np.sqrt(0.5))))


def _store_values_with_ones(v, ve_ref):
    ve_ref[0, :, :LANES] = v.astype(BF16)
    ve_ref[0, :, LANES:] = jnp.ones(v.shape, BF16)


def _pre0_kernel(x_ref, mod_ref, gmix_ref, win_ref, cos_ref, sin_ref, qg_ref, kg_ref, gv_ref,
                 bsum_ref, ws_ref, bs_ref, q_ref, kt_ref, v_ref, gm_ref):
    tm = x_ref.shape[1]
    m = mod_ref[0]
    h = _norm_modulate(x_ref[0], gmix_ref[...], m[0:1], m[1:2])
    proj = jnp.dot(h.astype(BF16), win_ref[...], preferred_element_type=F32)

    lane = lax.broadcasted_iota(jnp.int32, (tm, LANES), 1)
    even_lane = (lane % 2) == 0
    cos = cos_ref[...]
    sin_signed = sin_ref[...]

    def head_norm_rope(t, mean_sq, gain):
        return _rope((t * lax.rsqrt(mean_sq + EPS)) * gain, cos, sin_signed, even_lane)

    u0 = ATTN_WIDTH + 2 * KV_WIDTH
    g0 = u0 + GMLP_WIDTH
    n_gm = GMLP_WIDTH // LANES
    blocks = [proj[:, g * LANES:(g + 1) * LANES] for g in range(GQA_GROUP)]
    blocks.append(proj[:, ATTN_WIDTH:ATTN_WIDTH + KV_WIDTH])
    blocks += [_gelu(proj[:, g0 + j * LANES:g0 + (j + 1) * LANES]) for j in range(n_gm)]
    mean_sq = []
    for a in range(0, len(blocks), 2):
        pair = _seg_mean_sq(blocks[a], blocks[min(a + 1, len(blocks) - 1)], bsum_ref)
        mean_sq += list(pair)

    for g in range(GQA_GROUP):
        q_ref[0, :, g * LANES:(g + 1) * LANES] = (
            head_norm_rope(blocks[g], mean_sq[g], qg_ref[...]) * Q_SCALE).astype(BF16)
    k = head_norm_rope(blocks[GQA_GROUP], mean_sq[GQA_GROUP], kg_ref[...])
    kt_ref[0] = k.T.astype(BF16)
    _store_values_with_ones(proj[:, ATTN_WIDTH + KV_WIDTH:ATTN_WIDTH + 2 * KV_WIDTH], v_ref)

    left = lax.broadcasted_iota(jnp.int32, (CHUNK, LANES), 1) < GMLP_GROUP_DIM
    for j in range(n_gm):
        u = _gelu(proj[:, u0 + j * LANES:u0 + (j + 1) * LANES])
        vv = blocks[GQA_GROUP + 1 + j]
        vg = (vv * lax.rsqrt(mean_sq[GQA_GROUP + 1 + j] + EPS)) * gv_ref[:, j * LANES:(j + 1) * LANES]
        bias = bs_ref[:, j * LANES:(j + 1) * LANES]
        for n in range(tm // CHUNK):
            blk = vg[n * CHUNK:(n + 1) * CHUNK]
            rhs = jnp.concatenate([jnp.where(left, blk, 0.0), jnp.where(left, 0.0, blk)],
                                  axis=0).astype(BF16)
            mixed = jnp.dot(ws_ref[j], rhs, preferred_element_type=F32) + bias
            gm_ref[0, n * CHUNK:(n + 1) * CHUNK, j * LANES:(j + 1) * LANES] = (
                u[n * CHUNK:(n + 1) * CHUNK] * mixed).astype(BF16)


def _pre0(x, mod0, g_mix0, w_in_b, cos_t, sin_t, qg, kg, gv, bsum, ws2, bs_t, tm):
    b, s, d = x.shape
    nw = w_in_b.shape[1]
    return pl.pallas_call(
        _pre0_kernel,
        grid=(b, s // tm),
        in_specs=[
            pl.BlockSpec((1, tm, d), lambda bi, i: (bi, i, 0)),
            pl.BlockSpec((1, 6, d), lambda bi, i: (bi, 0, 0)),
            _const_spec((1, d)),
            _const_spec((d, nw)),
            pl.BlockSpec((tm, LANES), lambda bi, i: (i, 0)),
            pl.BlockSpec((tm, LANES), lambda bi, i: (i, 0)),
            _const_spec((1, LANES)),
            _const_spec((1, LANES)),
            _const_spec((1, GMLP_WIDTH)),
            _const_spec((2 * LANES, 2 * LANES)),
            _const_spec(ws2.shape),
            _const_spec(bs_t.shape),
        ],
        out_specs=[
            pl.BlockSpec((1, tm, ATTN_WIDTH), lambda bi, i: (bi, i, 0)),
            pl.BlockSpec((1, KV_WIDTH, tm), lambda bi, i: (bi, 0, i)),
            pl.BlockSpec((1, tm, 2 * LANES), lambda bi, i: (bi, i, 0)),
            pl.BlockSpec((1, tm, GMLP_WIDTH), lambda bi, i: (bi, i, 0)),
        ],
        out_shape=[
            jax.ShapeDtypeStruct((b, s, ATTN_WIDTH), BF16),
            jax.ShapeDtypeStruct((b, KV_WIDTH, s), BF16),
            jax.ShapeDtypeStruct((b, s, 2 * LANES), BF16),
            jax.ShapeDtypeStruct((b, s, GMLP_WIDTH), BF16),
        ],
        compiler_params=_params(2),
        name="pre0",
    )(x, mod0, g_mix0, w_in_b, cos_t, sin_t, qg, kg, gv, bsum, ws2, bs_t)


def _ctx_kernel(c_ref, mod_ref, gmix_ref, wkv_ref, kg_ref, bsum_ref, kt_ref, v_ref):
    m = mod_ref[0]
    h = _norm_modulate(c_ref[0], gmix_ref[...], m[0:1], m[1:2])
    proj = jnp.dot(h.astype(BF16), wkv_ref[...], preferred_element_type=F32)
    k = proj[:, :KV_WIDTH]
    k = (k * lax.rsqrt(_seg_mean_sq(k, k, bsum_ref)[0] + EPS)) * kg_ref[...]
    kt_ref[0] = k.T.astype(BF16)
    _store_values_with_ones(proj[:, KV_WIDTH:], v_ref)


def _ctx_kv(ctx, mod0, g_mix0, w_kv_b, kg, bsum, ctx_row):
    b, n, d = ctx.shape
    return pl.pallas_call(
        _ctx_kernel,
        grid=(b,),
        in_specs=[
            pl.BlockSpec((1, n, d), lambda bi: (bi, 0, 0)),
            pl.BlockSpec((1, 6, d), lambda bi: (ctx_row, 0, 0)),
            _const_spec((1, d)),
            _const_spec((d, 2 * KV_WIDTH)),
            _const_spec((1, LANES)),
            _const_spec((2 * LANES, 2 * LANES)),
        ],
        out_specs=[
            pl.BlockSpec((1, KV_WIDTH, n), lambda bi: (bi, 0, 0)),
            pl.BlockSpec((1, n, 2 * LANES), lambda bi: (bi, 0, 0)),
        ],
        out_shape=[
            jax.ShapeDtypeStruct((b, KV_WIDTH, n), BF16),
            jax.ShapeDtypeStruct((b, n, 2 * LANES), BF16),
        ],
        compiler_params=_params(1),
        name="ctx_kv",
    )(ctx, mod0, g_mix0, w_kv_b, kg, bsum)


def _attn_kernel(q_ref, kt_ref, ve_ref, kct_ref, vce_ref, qg_ref, kg_ref, o_ref, qz_ref, r_ref):
    tq = q_ref.shape[1]
    n_heads = GQA_GROUP * N_KV_HEADS
    left = lax.broadcasted_iota(jnp.int32, (tq, LANES), 1) < HEAD_DIM
    m = (Q_SCALE * HEAD_DIM) * jnp.max(jnp.abs(qg_ref[...])) * jnp.max(jnp.abs(kg_ref[...]))

    for g in range(GQA_GROUP):
        qc = q_ref[0, :, g * LANES:(g + 1) * LANES].astype(F32)
        qz_ref[N_KV_HEADS * g] = jnp.where(left, qc, 0.0).astype(BF16)
        qz_ref[N_KV_HEADS * g + 1] = jnp.where(left, 0.0, qc).astype(BF16)

    def finish(r, u):
        r_ref[u] = r[:, :LANES] / r[:, LANES:]

    def key_tiles():
        for k_ref, v_ref in ((kt_ref, ve_ref), (kct_ref, vce_ref)):
            for j in range(k_ref.shape[2] // KEY_TILE):
                yield (k_ref.at[0, :, j * KEY_TILE:(j + 1) * KEY_TILE],
                       v_ref.at[0, j * KEY_TILE:(j + 1) * KEY_TILE, :])

    def streamed():
        for u in range(n_heads):
            qz = qz_ref[u]
            r = None
            for k_tile, v_tile in key_tiles():
                s = jnp.dot(qz, k_tile[...], preferred_element_type=F32)
                p = jnp.exp2(s - m).astype(BF16)
                d = jnp.dot(p, v_tile[...], preferred_element_type=F32)
                r = d if r is None else r + d
            finish(r, u)

    def exact_max():
        def head(u, carry):
            s1 = jnp.dot(qz_ref[u], kt_ref[0], preferred_element_type=F32)
            s2 = jnp.dot(qz_ref[u], kct_ref[0], preferred_element_type=F32)
            mx = jnp.maximum(jnp.max(s1, axis=-1, keepdims=True), jnp.max(s2, axis=-1, keepdims=True))
            finish(jnp.dot(jnp.exp2(s1 - mx).astype(BF16), ve_ref[0], preferred_element_type=F32)
                   + jnp.dot(jnp.exp2(s2 - mx).astype(BF16), vce_ref[0], preferred_element_type=F32), u)
            return carry
        lax.fori_loop(0, n_heads, head, 0)

    lax.cond(m <= MAX_SAFE_SHIFT, streamed, exact_max)

    for g in range(GQA_GROUP):
        o_ref[0, :, g * LANES:(g + 1) * LANES] = jnp.where(
            left, r_ref[N_KV_HEADS * g], r_ref[N_KV_HEADS * g + 1]).astype(BF16)


def _attention(q, kt, ve, kct, vce, qg, kg, tq):
    b, s, _ = q.shape
    nc = kct.shape[2]
    n_heads = GQA_GROUP * N_KV_HEADS
    assert s % KEY_TILE == 0 and nc % KEY_TILE == 0
    return pl.pallas_call(
        _attn_kernel,
        grid=(b, s // tq),
        in_specs=[
            pl.BlockSpec((1, tq, ATTN_WIDTH), lambda bi, i: (bi, i, 0)),
            pl.BlockSpec((1, KV_WIDTH, s), lambda bi, i: (bi, 0, 0)),
            pl.BlockSpec((1, s, 2 * LANES), lambda bi, i: (bi, 0, 0)),
            pl.BlockSpec((1, KV_WIDTH, nc), lambda bi, i: (bi, 0, 0)),
            pl.BlockSpec((1, nc, 2 * LANES), lambda bi, i: (bi, 0, 0)),
            _const_spec((1, LANES)),
            _const_spec((1, LANES)),
        ],
        out_specs=pl.BlockSpec((1, tq, ATTN_WIDTH), lambda bi, i: (bi, i, 0)),
        out_shape=jax.ShapeDtypeStruct((b, s, ATTN_WIDTH), BF16),
        scratch_shapes=[
            pltpu.VMEM((n_heads, tq, LANES), BF16),
            pltpu.VMEM((n_heads, tq, LANES), F32),
        ],
        compiler_params=_params(2),
        name="attention",
    )(q, kt, ve, kct, vce, qg, kg)


def _post0_kernel(x_ref, a_ref, gm_ref, mod_ref, gffn_ref, woa_ref, wog_ref, w1_ref, w3_ref, w2_ref,
                  o_ref):
    m = mod_ref[0]
    mix = (jnp.dot(a_ref[0], woa_ref[...], preferred_element_type=F32)
           + jnp.dot(gm_ref[0], wog_ref[...], preferred_element_type=F32))
    x1 = x_ref[0] + m[2:3] * mix
    h = _norm_modulate(x1, gffn_ref[...], m[3:4], m[4:5])
    o_ref[0] = x1 + m[5:6] * _swiglu(h.astype(BF16), w1_ref, w3_ref, w2_ref)


def _post0(x, attn, gm, mod0, g_ffn0, woa, wog, w1, w3, w2, layer, tm):
    b, s, d = x.shape
    dff = w1.shape[2]
    return pl.pallas_call(
        _post0_kernel,
        grid=(b, s // tm),
        in_specs=[
            pl.BlockSpec((1, tm, d), lambda bi, i: (bi, i, 0)),
            pl.BlockSpec((1, tm, ATTN_WIDTH), lambda bi, i: (bi, i, 0)),
            pl.BlockSpec((1, tm, GMLP_WIDTH), lambda bi, i: (bi, i, 0)),
            pl.BlockSpec((1, 6, d), lambda bi, i: (bi, 0, 0)),
            _const_spec((1, d)),
            _const_spec((ATTN_WIDTH, d)),
            _const_spec((GMLP_WIDTH, d)),
            _layer_spec((d, dff), layer),
            _layer_spec((d, dff), layer),
            _layer_spec((dff, d), layer),
        ],
        out_specs=pl.BlockSpec((1, tm, d), lambda bi, i: (bi, i, 0)),
        out_shape=jax.ShapeDtypeStruct((b, s, d), F32),
        compiler_params=_params(2),
        name="post0",
    )(x, attn, gm, mod0, g_ffn0, woa, wog, w1, w3, w2)


def _layer1_kernel(x_ref, prev_ref, next_ref, mod_ref, gmix_ref, gffn_ref, ps_ref, band_ref, wp_ref,
                   w1_ref, w3_ref, w2_ref, gfin_ref, o_ref, *, seq_len):
    tm = x_ref.shape[1]
    i = pl.program_id(1)
    m = mod_ref[0]

    def norm_mod(t):
        return _norm_modulate(t, gmix_ref[...], m[0:1], m[1:2])

    x = x_ref[0]
    xn = norm_mod(x)
    xp = jnp.where(i > 0, norm_mod(prev_ref[0]), 0.0)
    xq = jnp.where(i < pl.num_programs(1) - 1, norm_mod(next_ref[0]), 0.0)
    ext = jnp.concatenate([xp, xn, xq], axis=0)
    ext_hi = ext.astype(BF16)
    ext_lo = (ext - ext_hi.astype(F32)).astype(BF16)

    sub = band_ref.shape[1]
    pos = i * tm + lax.broadcasted_iota(jnp.int32, (tm, LANES), 0)
    ys = []
    for gi, w in enumerate(POOL_WINDOWS):
        left_w = w // 2
        right_w = w - 1 - left_w
        cnt = (jnp.minimum(pos + right_w + 1, seq_len) - jnp.maximum(pos - left_w, 0)).astype(F32)
        inv_cnt = jnp.concatenate([1.0 / cnt] * (POOL_GROUP_DIM // LANES), axis=1)
        sl = slice(gi * POOL_GROUP_DIM, (gi + 1) * POOL_GROUP_DIM)
        sums = []
        for r in range(tm // sub):
            rows = slice(r * sub, r * sub + sub + 2 * HALO)
            hi_lo = jnp.concatenate([ext_hi[rows, sl], ext_lo[rows, sl]], axis=0)
            sums.append(jnp.dot(band_ref[gi], hi_lo, preferred_element_type=F32))
        pooled = jnp.concatenate(sums, axis=0) * inv_cnt - xn[:, sl]
        ys.append(jnp.dot(pooled.astype(BF16), wp_ref[gi], preferred_element_type=F32))
    y = jnp.concatenate(ys, axis=1) * ps_ref[...]
    x1 = x + m[2:3] * y
    h = _norm_modulate(x1, gffn_ref[...], m[3:4], m[4:5])
    x2 = x1 + m[5:6] * _swiglu(h.astype(BF16), w1_ref, w3_ref, w2_ref)
    o_ref[0] = _rms_rows(x2, gfin_ref[...])


def _layer1(x, mod1, g_mix1, g_ffn1, pool_scale, band, wp, w1, w3, w2, g_final, layer, tm):
    b, s, d = x.shape
    dff = w1.shape[2]
    per = tm // HALO
    last = s // HALO - 1
    return pl.pallas_call(
        functools.partial(_layer1_kernel, seq_len=s),
        grid=(b, s // tm),
        in_specs=[
            pl.BlockSpec((1, tm, d), lambda bi, i: (bi, i, 0)),
            pl.BlockSpec((1, HALO, d), lambda bi, i: (bi, jnp.maximum(i * per - 1, 0), 0)),
            pl.BlockSpec((1, HALO, d), lambda bi, i: (bi, jnp.minimum((i + 1) * per, last), 0)),
            pl.BlockSpec((1, 6, d), lambda bi, i: (bi, 0, 0)),
            _const_spec((1, d)),
            _const_spec((1, d)),
            _const_spec((1, d)),
            _const_spec(band.shape),
            _const_spec(wp.shape),
            _layer_spec((d, dff), layer),
            _layer_spec((d, dff), layer),
            _layer_spec((dff, d), layer),
            _const_spec((1, d)),
        ],
        out_specs=pl.BlockSpec((1, tm, d), lambda bi, i: (bi, i, 0)),
        out_shape=jax.ShapeDtypeStruct((b, s, d), F32),
        compiler_params=_params(2),
        name="layer1",
    )(x, x, x, mod1, g_mix1, g_ffn1, pool_scale, band, wp, w1, w3, w2, g_final)


def _rope_tables(n):
    rows = n // GRID_W
    row = jnp.repeat(jnp.arange(rows), GRID_W).astype(F32)
    col = jnp.tile(jnp.arange(GRID_W), rows).astype(F32)
    half = HEAD_DIM // 2
    freqs = ROPE_THETA ** (-jnp.arange(0, half, 2, dtype=F32) / half)
    ang = jnp.concatenate([row[:, None] * freqs, col[:, None] * freqs], axis=-1)
    cos = jnp.tile(jnp.repeat(jnp.cos(ang), 2, axis=1), (1, LANES // HEAD_DIM))
    sin = jnp.tile(jnp.repeat(jnp.sin(ang), 2, axis=1), (1, LANES // HEAD_DIM))
    sign = jnp.where(jnp.arange(LANES) % 2 == 0, -1.0, 1.0).astype(F32)
    return cos, sin * sign


def _band_matrices(sub):
    t = np.arange(sub)[:, None]
    e = np.arange(sub + 2 * HALO)[None, :]
    mats = []
    for w in POOL_WINDOWS:
        left = w // 2
        right = w - 1 - left
        member = ((e >= t + HALO - left) & (e <= t + HALO + right)).astype(np.float32)
        mats.append(np.concatenate([member, member], axis=1))
    return jnp.asarray(np.stack(mats), dtype=BF16)


def kernel(x, c, ctx, c_ctx, w_ada, b_ada, g_mix, g_ffn, w_in, w_out, q_norm, k_norm, gmlp_norm,
           w_spatial, b_spatial, w_pool, pool_scale, w1, w3, w2, g_final):
    b, s, d = x.shape
    depth = w_ada.shape[0]
    assert depth == 2 and d == D_MODEL and s % CHUNK == 0
    tm_pre, tq, tm_ffn = 1024, 512, 512

    cond = jnp.concatenate([c, c_ctx[None], jnp.zeros((COND_ROWS - b - 1, d), F32)], axis=0)
    mods = _adaln(cond, w_ada, b_ada).reshape(depth, COND_ROWS, 6, d)

    wi = w_in[0]
    wq = wi[:, :ATTN_WIDTH].reshape(d, N_KV_HEADS, GQA_GROUP, HEAD_DIM).transpose(0, 2, 1, 3)
    w_in_b = jnp.concatenate([wq.reshape(d, ATTN_WIDTH), wi[:, ATTN_WIDTH:]], axis=1).astype(BF16)
    w_kv_b = wi[:, ATTN_WIDTH:ATTN_WIDTH + 2 * KV_WIDTH].astype(BF16)
    wo = w_out[0]
    woa = wo[:ATTN_WIDTH].reshape(N_KV_HEADS, GQA_GROUP, HEAD_DIM, d).transpose(1, 0, 2, 3)
    woa = woa.reshape(ATTN_WIDTH, d).astype(BF16)
    wog = wo[ATTN_WIDTH:].astype(BF16)
    cos_t, sin_t = _rope_tables(s)
    qg = jnp.tile(q_norm[0], LANES // HEAD_DIM)[None]
    kg = jnp.tile(k_norm[0], LANES // HEAD_DIM)[None]
    gv = gmlp_norm[0].reshape(1, GMLP_WIDTH)
    seg = np.arange(2 * LANES) // HEAD_DIM
    bsum = jnp.asarray((seg[:, None] == seg[None, :]).astype(np.float32), dtype=BF16)
    ws = w_spatial[0]
    ws2 = jnp.concatenate([ws[0::2], ws[1::2]], axis=2).astype(BF16)
    bs_t = jnp.repeat(b_spatial[0].T, GMLP_GROUP_DIM, axis=1)
    w1b, w3b, w2b = w1.astype(BF16), w3.astype(BF16), w2.astype(BF16)

    q, kt, v, gm = _pre0(x, mods[0], g_mix[0][None], w_in_b, cos_t, sin_t, qg, kg, gv, bsum, ws2,
                         bs_t, tm_pre)
    kct, vc = _ctx_kv(ctx, mods[0], g_mix[0][None], w_kv_b, kg, bsum, b)
    attn = _attention(q, kt, v, kct, vc, qg, kg, tq)
    x1 = _post0(x, attn, gm, mods[0], g_ffn[0][None], woa, wog, w1b, w3b, w2b, 0, tm_ffn)

    band = _band_matrices(POOL_SUB)
    return _layer1(x1, mods[1], g_mix[1][None], g_ffn[1][None], pool_scale[0][None], band,
                   w_pool[0].astype(BF16), w1b, w3b, w2b, g_final[None], 1, tm_ffn)
```

```python
import functools

import numpy as np
import jax
import jax.numpy as jnp
from jax import lax
from jax.experimental import pallas as pl
from jax.experimental.pallas import tpu as pltpu

D_MODEL = 1024
GRID_W = 64
N_HEADS = 8
N_KV_HEADS = 2
HEAD_DIM = 64
GQA_GROUP = N_HEADS // N_KV_HEADS
ATTN_WIDTH = N_HEADS * HEAD_DIM
KV_WIDTH = N_KV_HEADS * HEAD_DIM
ROPE_THETA = 10000.0
GMLP_GROUPS = 8
GMLP_GROUP_DIM = 64
GMLP_WIDTH = GMLP_GROUPS * GMLP_GROUP_DIM
CHUNK = 128
POOL_WINDOWS = (2, 4, 8, 16)
POOL_GROUP_DIM = D_MODEL // len(POOL_WINDOWS)
EPS = 1e-6
Q_SCALE = float(HEAD_DIM ** -0.5 * np.log2(np.e))

LANES = 128
HALO = 8
COND_ROWS = 16
VMEM_LIMIT = 56 * 1024 * 1024
POOL_SUB = 64
KEY_TILE = 256
MAX_SAFE_SHIFT = 60.0

F32 = jnp.float32
BF16 = jnp.bfloat16


def _const_spec(shape):
    nd = len(shape)
    return pl.BlockSpec(shape, lambda *_: (0,) * nd, pipeline_mode=pl.Buffered(1))


def _layer_spec(shape, layer):
    nd = len(shape)
    return pl.BlockSpec((None,) + tuple(shape), lambda *_: (layer,) + (0,) * nd,
                        pipeline_mode=pl.Buffered(1))


def _params(n_axes):
    return pltpu.CompilerParams(dimension_semantics=("arbitrary",) * n_axes,
                                vmem_limit_bytes=VMEM_LIMIT)


def _rms_rows(x, gain):
    ms = jnp.mean(x * x, axis=-1, keepdims=True)
    return (x * lax.rsqrt(ms + EPS)) * gain


def _norm_modulate(x, gain, shift, scale):
    return _rms_rows(x, gain * (1.0 + scale)) + shift


def _seg_mean_sq(ta, tb, bsum_ref):
    sq = jnp.concatenate([(ta * ta).astype(BF16), (tb * tb).astype(BF16)], axis=1)
    ss = jnp.dot(sq, bsum_ref[...], preferred_element_type=F32) * (1.0 / HEAD_DIM)
    return ss[:, :LANES], ss[:, LANES:]


def _swiglu(h, w1_ref, w3_ref, w2_ref):
    a = jnp.dot(h, w1_ref[...], preferred_element_type=F32)
    b = jnp.dot(h, w3_ref[...], preferred_element_type=F32)
    g = (a * jax.nn.sigmoid(a) * b).astype(BF16)
    return jnp.dot(g, w2_ref[...], preferred_element_type=F32)


def _adaln_kernel(cond_ref, w_ref, b_ref, o_ref):
    s = cond_ref[...]
    s = (s * jax.nn.sigmoid(s)).astype(BF16)
    o_ref[0] = jnp.dot(s, w_ref[0].astype(BF16), preferred_element_type=F32) + b_ref[0]


def _adaln(cond, w_ada, b_ada, tn=1536):
    depth, d, n = w_ada.shape
    return pl.pallas_call(
        _adaln_kernel,
        grid=(depth, n // tn),
        in_specs=[
            pl.BlockSpec((COND_ROWS, d), lambda l, j: (0, 0)),
            pl.BlockSpec((1, d, tn), lambda l, j: (l, 0, j)),
            pl.BlockSpec((1, 1, tn), lambda l, j: (l, 0, j)),
        ],
        out_specs=pl.BlockSpec((1, COND_ROWS, tn), lambda l, j: (l, 0, j)),
        out_shape=jax.ShapeDtypeStruct((depth, COND_ROWS, n), F32),
        compiler_params=_params(2),
        name="adaln",
    )(cond, w_ada, b_ada.reshape(depth, 1, n))


def _rope(t, cos, sin_signed, even_lane):
    partner = jnp.where(even_lane, pltpu.roll(t, LANES - 1, 1), pltpu.roll(t, 1, 1))
    return t * cos + partner * sin_signed


def _gelu(x):
    return 0.5 * x * (1.0 + lax.erf(x * np.float32(np.sqrt(0.5))))


def _store_values_with_ones(v, ve_ref):
    ve_ref[0, :, :LANES] = v.astype(BF16)
    ve_ref[0, :, LANES:] = jnp.ones(v.shape, BF16)


def _pre0_kernel(x_ref, mod_ref, gmix_ref, win_ref, cos_ref, sin_ref, qg_ref, kg_ref, gv_ref,
                 bsum_ref, ws_ref, bs_ref, q_ref, kt_ref, v_ref, gm_ref):
    tm = x_ref.shape[1]
    m = mod_ref[0]
    h = _norm_modulate(x_ref[0], gmix_ref[...], m[0:1], m[1:2])
    proj = jnp.dot(h.astype(BF16), win_ref[...], preferred_element_type=F32)

    lane = lax.broadcasted_iota(jnp.int32, (tm, LANES), 1)
    even_lane = (lane % 2) == 0
    cos = cos_ref[...]
    sin_signed = sin_ref[...]

    def head_norm_rope(t, mean_sq, gain):
        return _rope((t * lax.rsqrt(mean_sq + EPS)) * gain, cos, sin_signed, even_lane)

    u0 = ATTN_WIDTH + 2 * KV_WIDTH
    g0 = u0 + GMLP_WIDTH
    n_gm = GMLP_WIDTH // LANES
    blocks = [proj[:, g * LANES:(g + 1) * LANES] for g in range(GQA_GROUP)]
    blocks.append(proj[:, ATTN_WIDTH:ATTN_WIDTH + KV_WIDTH])
    blocks += [_gelu(proj[:, g0 + j * LANES:g0 + (j + 1) * LANES]) for j in range(n_gm)]
    mean_sq = []
    for a in range(0, len(blocks), 2):
        pair = _seg_mean_sq(blocks[a], blocks[min(a + 1, len(blocks) - 1)], bsum_ref)
        mean_sq += list(pair)

    for g in range(GQA_GROUP):
        q_ref[0, :, g * LANES:(g + 1) * LANES] = (
            head_norm_rope(blocks[g], mean_sq[g], qg_ref[...]) * Q_SCALE).astype(BF16)
    k = head_norm_rope(blocks[GQA_GROUP], mean_sq[GQA_GROUP], kg_ref[...])
    kt_ref[0] = k.T.astype(BF16)
    _store_values_with_ones(proj[:, ATTN_WIDTH + KV_WIDTH:ATTN_WIDTH + 2 * KV_WIDTH], v_ref)

    left = lax.broadcasted_iota(jnp.int32, (CHUNK, LANES), 1) < GMLP_GROUP_DIM
    for j in range(n_gm):
        u = _gelu(proj[:, u0 + j * LANES:u0 + (j + 1) * LANES])
        vv = blocks[GQA_GROUP + 1 + j]
        vg = (vv * lax.rsqrt(mean_sq[GQA_GROUP + 1 + j] + EPS)) * gv_ref[:, j * LANES:(j + 1) * LANES]
        bias = bs_ref[:, j * LANES:(j + 1) * LANES]
        for n in range(tm // CHUNK):
            blk = vg[n * CHUNK:(n + 1) * CHUNK]
            rhs = jnp.concatenate([jnp.where(left, blk, 0.0), jnp.where(left, 0.0, blk)],
                                  axis=0).astype(BF16)
            mixed = jnp.dot(ws_ref[j], rhs, preferred_element_type=F32) + bias
            gm_ref[0, n * CHUNK:(n + 1) * CHUNK, j * LANES:(j + 1) * LANES] = (
                u[n * CHUNK:(n + 1) * CHUNK] * mixed).astype(BF16)


def _pre0(x, mod0, g_mix0, w_in_b, cos_t, sin_t, qg, kg, gv, bsum, ws2, bs_t, tm):
    b, s, d = x.shape
    nw = w_in_b.shape[1]
    return pl.pallas_call(
        _pre0_kernel,
        grid=(b, s // tm),
        in_specs=[
            pl.BlockSpec((1, tm, d), lambda bi, i: (bi, i, 0)),
            pl.BlockSpec((1, 6, d), lambda bi, i: (bi, 0, 0)),
            _const_spec((1, d)),
            _const_spec((d, nw)),
            pl.BlockSpec((tm, LANES), lambda bi, i: (i, 0)),
            pl.BlockSpec((tm, LANES), lambda bi, i: (i, 0)),
            _const_spec((1, LANES)),
            _const_spec((1, LANES)),
            _const_spec((1, GMLP_WIDTH)),
            _const_spec((2 * LANES, 2 * LANES)),
            _const_spec(ws2.shape),
            _const_spec(bs_t.shape),
        ],
        out_specs=[
            pl.BlockSpec((1, tm, ATTN_WIDTH), lambda bi, i: (bi, i, 0)),
            pl.BlockSpec((1, KV_WIDTH, tm), lambda bi, i: (bi, 0, i)),
            pl.BlockSpec((1, tm, 2 * LANES), lambda bi, i: (bi, i, 0)),
            pl.BlockSpec((1, tm, GMLP_WIDTH), lambda bi, i: (bi, i, 0)),
        ],
        out_shape=[
            jax.ShapeDtypeStruct((b, s, ATTN_WIDTH), BF16),
            jax.ShapeDtypeStruct((b, KV_WIDTH, s), BF16),
            jax.ShapeDtypeStruct((b, s, 2 * LANES), BF16),
            jax.ShapeDtypeStruct((b, s, GMLP_WIDTH), BF16),
        ],
        compiler_params=_params(2),
        name="pre0",
    )(x, mod0, g_mix0, w_in_b, cos_t, sin_t, qg, kg, gv, bsum, ws2, bs_t)


def _ctx_kernel(c_ref, mod_ref, gmix_ref, wkv_ref, kg_ref, bsum_ref, kt_ref, v_ref):
    m = mod_ref[0]
    h = _norm_modulate(c_ref[0], gmix_ref[...], m[0:1], m[1:2])
    proj = jnp.dot(h.astype(BF16), wkv_ref[...], preferred_element_type=F32)
    k = proj[:, :KV_WIDTH]
    k = (k * lax.rsqrt(_seg_mean_sq(k, k, bsum_ref)[0] + EPS)) * kg_ref[...]
    kt_ref[0] = k.T.astype(BF16)
    _store_values_with_ones(proj[:, KV_WIDTH:], v_ref)


def _ctx_kv(ctx, mod0, g_mix0, w_kv_b, kg, bsum, ctx_row):
    b, n, d = ctx.shape
    return pl.pallas_call(
        _ctx_kernel,
        grid=(b,),
        in_specs=[
            pl.BlockSpec((1, n, d), lambda bi: (bi, 0, 0)),
            pl.BlockSpec((1, 6, d), lambda bi: (ctx_row, 0, 0)),
            _const_spec((1, d)),
            _const_spec((d, 2 * KV_WIDTH)),
            _const_spec((1, LANES)),
            _const_spec((2 * LANES, 2 * LANES)),
        ],
        out_specs=[
            pl.BlockSpec((1, KV_WIDTH, n), lambda bi: (bi, 0, 0)),
            pl.BlockSpec((1, n, 2 * LANES), lambda bi: (bi, 0, 0)),
        ],
        out_shape=[
            jax.ShapeDtypeStruct((b, KV_WIDTH, n), BF16),
            jax.ShapeDtypeStruct((b, n, 2 * LANES), BF16),
        ],
        compiler_params=_params(1),
        name="ctx_kv",
    )(ctx, mod0, g_mix0, w_kv_b, kg, bsum)


def _attn_kernel(q_ref, kt_ref, ve_ref, kct_ref, vce_ref, qg_ref, kg_ref, o_ref, qz_ref, r_ref):
    tq = q_ref.shape[1]
    n_heads = GQA_GROUP * N_KV_HEADS
    left = lax.broadcasted_iota(jnp.int32, (tq, LANES), 1) < HEAD_DIM
    m = (Q_SCALE * HEAD_DIM) * jnp.max(jnp.abs(qg_ref[...])) * jnp.max(jnp.abs(kg_ref[...]))

    def masked_queries(g):
        qc = q_ref[0, :, g * LANES:(g + 1) * LANES].astype(F32)
        return jnp.where(left, qc, 0.0).astype(BF16), jnp.where(left, 0.0, qc).astype(BF16)

    def normalised(r):
        return r[:, :LANES] / r[:, LANES:]

    def store_group(g, o_kh0, o_kh1):
        o_ref[0, :, g * LANES:(g + 1) * LANES] = jnp.where(left, o_kh0, o_kh1).astype(BF16)

    def key_tiles():
        for k_ref, v_ref in ((kt_ref, ve_ref), (kct_ref, vce_ref)):
            for j in range(k_ref.shape[2] // KEY_TILE):
                yield (k_ref.at[0, :, j * KEY_TILE:(j + 1) * KEY_TILE],
                       v_ref.at[0, j * KEY_TILE:(j + 1) * KEY_TILE, :])

    def streamed():
        for g in range(GQA_GROUP):
            outs = []
            for qz in masked_queries(g):
                r = None
                for k_tile, v_tile in key_tiles():
                    s = jnp.dot(qz, k_tile[...], preferred_element_type=F32)
                    p = jnp.exp2(s - m).astype(BF16)
                    d = jnp.dot(p, v_tile[...], preferred_element_type=F32)
                    r = d if r is None else r + d
                outs.append(normalised(r))
            store_group(g, *outs)

    def exact_max():
        for g in range(GQA_GROUP):
            qz_ref[N_KV_HEADS * g], qz_ref[N_KV_HEADS * g + 1] = masked_queries(g)

        def head(u, carry):
            s1 = jnp.dot(qz_ref[u], kt_ref[0], preferred_element_type=F32)
            s2 = jnp.dot(qz_ref[u], kct_ref[0], preferred_element_type=F32)
            mx = jnp.maximum(jnp.max(s1, axis=-1, keepdims=True), jnp.max(s2, axis=-1, keepdims=True))
            r_ref[u] = normalised(
                jnp.dot(jnp.exp2(s1 - mx).astype(BF16), ve_ref[0], preferred_element_type=F32)
                + jnp.dot(jnp.exp2(s2 - mx).astype(BF16), vce_ref[0], preferred_element_type=F32))
            return carry
        lax.fori_loop(0, n_heads, head, 0)
        for g in range(GQA_GROUP):
            store_group(g, r_ref[N_KV_HEADS * g], r_ref[N_KV_HEADS * g + 1])

    lax.cond(m <= MAX_SAFE_SHIFT, streamed, exact_max)


def _attention(q, kt, ve, kct, vce, qg, kg, tq):
    b, s, _ = q.shape
    nc = kct.shape[2]
    n_heads = GQA_GROUP * N_KV_HEADS
    assert s % KEY_TILE == 0 and nc % KEY_TILE == 0
    return pl.pallas_call(
        _attn_kernel,
        grid=(b, s // tq),
        in_specs=[
            pl.BlockSpec((1, tq, ATTN_WIDTH), lambda bi, i: (bi, i, 0)),
            pl.BlockSpec((1, KV_WIDTH, s), lambda bi, i: (bi, 0, 0)),
            pl.BlockSpec((1, s, 2 * LANES), lambda bi, i: (bi, 0, 0)),
            pl.BlockSpec((1, KV_WIDTH, nc), lambda bi, i: (bi, 0, 0)),
            pl.BlockSpec((1, nc, 2 * LANES), lambda bi, i: (bi, 0, 0)),
            _const_spec((1, LANES)),
            _const_spec((1, LANES)),
        ],
        out_specs=pl.BlockSpec((1, tq, ATTN_WIDTH), lambda bi, i: (bi, i, 0)),
        out_shape=jax.ShapeDtypeStruct((b, s, ATTN_WIDTH), BF16),
        scratch_shapes=[
            pltpu.VMEM((n_heads, tq, LANES), BF16),
            pltpu.VMEM((n_heads, tq, LANES), F32),
        ],
        compiler_params=_params(2),
        name="attention",
    )(q, kt, ve, kct, vce, qg, kg)


def _post0_kernel(x_ref, a_ref, gm_ref, mod_ref, gffn_ref, woa_ref, wog_ref, w1_ref, w3_ref, w2_ref,
                  o_ref):
    m = mod_ref[0]
    mix = (jnp.dot(a_ref[0], woa_ref[...], preferred_element_type=F32)
           + jnp.dot(gm_ref[0], wog_ref[...], preferred_element_type=F32))
    x1 = x_ref[0] + m[2:3] * mix
    h = _norm_modulate(x1, gffn_ref[...], m[3:4], m[4:5])
    o_ref[0] = x1 + m[5:6] * _swiglu(h.astype(BF16), w1_ref, w3_ref, w2_ref)


def _post0(x, attn, gm, mod0, g_ffn0, woa, wog, w1, w3, w2, layer, tm):
    b, s, d = x.shape
    dff = w1.shape[2]
    return pl.pallas_call(
        _post0_kernel,
        grid=(b, s // tm),
        in_specs=[
            pl.BlockSpec((1, tm, d), lambda bi, i: (bi, i, 0)),
            pl.BlockSpec((1, tm, ATTN_WIDTH), lambda bi, i: (bi, i, 0)),
            pl.BlockSpec((1, tm, GMLP_WIDTH), lambda bi, i: (bi, i, 0)),
            pl.BlockSpec((1, 6, d), lambda bi, i: (bi, 0, 0)),
            _const_spec((1, d)),
            _const_spec((ATTN_WIDTH, d)),
            _const_spec((GMLP_WIDTH, d)),
            _layer_spec((d, dff), layer),
            _layer_spec((d, dff), layer),
            _layer_spec((dff, d), layer),
        ],
        out_specs=pl.BlockSpec((1, tm, d), lambda bi, i: (bi, i, 0)),
        out_shape=jax.ShapeDtypeStruct((b, s, d), F32),
        compiler_params=_params(2),
        name="post0",
    )(x, attn, gm, mod0, g_ffn0, woa, wog, w1, w3, w2)


def _layer1_kernel(x_ref, prev_ref, next_ref, mod_ref, gmix_ref, gffn_ref, ps_ref, band_ref, wp_ref,
                   w1_ref, w3_ref, w2_ref, gfin_ref, o_ref, *, seq_len):
    tm = x_ref.shape[1]
    i = pl.program_id(1)
    m = mod_ref[0]

    def norm_mod(t):
        return _norm_modulate(t, gmix_ref[...], m[0:1], m[1:2])

    x = x_ref[0]
    xn = norm_mod(x)
    xp = jnp.where(i > 0, norm_mod(prev_ref[0]), 0.0)
    xq = jnp.where(i < pl.num_programs(1) - 1, norm_mod(next_ref[0]), 0.0)
    ext = jnp.concatenate([xp, xn, xq], axis=0)
    ext_hi = ext.astype(BF16)
    ext_lo = (ext - ext_hi.astype(F32)).astype(BF16)

    sub = band_ref.shape[1]
    pos = i * tm + lax.broadcasted_iota(jnp.int32, (tm, LANES), 0)
    ys = []
    for gi, w in enumerate(POOL_WINDOWS):
        left_w = w // 2
        right_w = w - 1 - left_w
        cnt = (jnp.minimum(pos + right_w + 1, seq_len) - jnp.maximum(pos - left_w, 0)).astype(F32)
        inv_cnt = jnp.concatenate([1.0 / cnt] * (POOL_GROUP_DIM // LANES), axis=1)
        sl = slice(gi * POOL_GROUP_DIM, (gi + 1) * POOL_GROUP_DIM)
        sums = []
        for r in range(tm // sub):
            rows = slice(r * sub, r * sub + sub + 2 * HALO)
            hi_lo = jnp.concatenate([ext_hi[rows, sl], ext_lo[rows, sl]], axis=0)
            sums.append(jnp.dot(band_ref[gi], hi_lo, preferred_element_type=F32))
        pooled = jnp.concatenate(sums, axis=0) * inv_cnt - xn[:, sl]
        ys.append(jnp.dot(pooled.astype(BF16), wp_ref[gi], preferred_element_type=F32))
    y = jnp.concatenate(ys, axis=1) * ps_ref[...]
    x1 = x + m[2:3] * y
    h = _norm_modulate(x1, gffn_ref[...], m[3:4], m[4:5])
    x2 = x1 + m[5:6] * _swiglu(h.astype(BF16), w1_ref, w3_ref, w2_ref)
    o_ref[0] = _rms_rows(x2, gfin_ref[...])


def _layer1(x, mod1, g_mix1, g_ffn1, pool_scale, band, wp, w1, w3, w2, g_final, layer, tm):
    b, s, d = x.shape
    dff = w1.shape[2]
    per = tm // HALO
    last = s // HALO - 1
    return pl.pallas_call(
        functools.partial(_layer1_kernel, seq_len=s),
        grid=(b, s // tm),
        in_specs=[
            pl.BlockSpec((1, tm, d), lambda bi, i: (bi, i, 0)),
            pl.BlockSpec((1, HALO, d), lambda bi, i: (bi, jnp.maximum(i * per - 1, 0), 0)),
            pl.BlockSpec((1, HALO, d), lambda bi, i: (bi, jnp.minimum((i + 1) * per, last), 0)),
            pl.BlockSpec((1, 6, d), lambda bi, i: (bi, 0, 0)),
            _const_spec((1, d)),
            _const_spec((1, d)),
            _const_spec((1, d)),
            _const_spec(band.shape),
            _const_spec(wp.shape),
            _layer_spec((d, dff), layer),
            _layer_spec((d, dff), layer),
            _layer_spec((dff, d), layer),
            _const_spec((1, d)),
        ],
        out_specs=pl.BlockSpec((1, tm, d), lambda bi, i: (bi, i, 0)),
        out_shape=jax.ShapeDtypeStruct((b, s, d), F32),
        compiler_params=_params(2),
        name="layer1",
    )(x, x, x, mod1, g_mix1, g_ffn1, pool_scale, band, wp, w1, w3, w2, g_final)


def _rope_tables(n):
    rows = n // GRID_W
    row = np.repeat(np.arange(rows), GRID_W).astype(np.float64)
    col = np.tile(np.arange(GRID_W), rows).astype(np.float64)
    half = HEAD_DIM // 2
    freqs = ROPE_THETA ** (-np.arange(0, half, 2, dtype=np.float64) / half)
    ang = np.concatenate([row[:, None] * freqs, col[:, None] * freqs], axis=-1)
    cos = np.tile(np.repeat(np.cos(ang), 2, axis=1), (1, LANES // HEAD_DIM))
    sin = np.tile(np.repeat(np.sin(ang), 2, axis=1), (1, LANES // HEAD_DIM))
    sign = np.where(np.arange(LANES) % 2 == 0, -1.0, 1.0)
    return jnp.asarray(cos, dtype=F32), jnp.asarray(sin * sign, dtype=F32)


def _band_matrices(sub):
    t = np.arange(sub)[:, None]
    e = np.arange(sub + 2 * HALO)[None, :]
    mats = []
    for w in POOL_WINDOWS:
        left = w // 2
        right = w - 1 - left
        member = ((e >= t + HALO - left) & (e <= t + HALO + right)).astype(np.float32)
        mats.append(np.concatenate([member, member], axis=1))
    return jnp.asarray(np.stack(mats), dtype=BF16)


def kernel(x, c, ctx, c_ctx, w_ada, b_ada, g_mix, g_ffn, w_in, w_out, q_norm, k_norm, gmlp_norm,
           w_spatial, b_spatial, w_pool, pool_scale, w1, w3, w2, g_final):
    b, s, d = x.shape
    depth = w_ada.shape[0]
    assert depth == 2 and d == D_MODEL and s % CHUNK == 0
    tm_pre, tq, tm_ffn = 1024, 512, 512

    cond = jnp.concatenate([c, c_ctx[None], jnp.zeros((COND_ROWS - b - 1, d), F32)], axis=0)
    mods = _adaln(cond, w_ada, b_ada).reshape(depth, COND_ROWS, 6, d)

    wi = w_in[0]
    wq = wi[:, :ATTN_WIDTH].reshape(d, N_KV_HEADS, GQA_GROUP, HEAD_DIM).transpose(0, 2, 1, 3)
    w_in_b = jnp.concatenate([wq.reshape(d, ATTN_WIDTH), wi[:, ATTN_WIDTH:]], axis=1).astype(BF16)
    w_kv_b = wi[:, ATTN_WIDTH:ATTN_WIDTH + 2 * KV_WIDTH].astype(BF16)
    wo = w_out[0]
    woa = wo[:ATTN_WIDTH].reshape(N_KV_HEADS, GQA_GROUP, HEAD_DIM, d).transpose(1, 0, 2, 3)
    woa = woa.reshape(ATTN_WIDTH, d).astype(BF16)
    wog = wo[ATTN_WIDTH:].astype(BF16)
    cos_t, sin_t = _rope_tables(s)
    qg = jnp.tile(q_norm[0], LANES // HEAD_DIM)[None]
    kg = jnp.tile(k_norm[0], LANES // HEAD_DIM)[None]
    gv = gmlp_norm[0].reshape(1, GMLP_WIDTH)
    seg = np.arange(2 * LANES) // HEAD_DIM
    bsum = jnp.asarray((seg[:, None] == seg[None, :]).astype(np.float32), dtype=BF16)
    ws = w_spatial[0]
    ws2 = jnp.concatenate([ws[0::2], ws[1::2]], axis=2).astype(BF16)
    bs_t = jnp.repeat(b_spatial[0].T, GMLP_GROUP_DIM, axis=1)
    w1b, w3b, w2b = w1.astype(BF16), w3.astype(BF16), w2.astype(BF16)

    q, kt, v, gm = _pre0(x, mods[0], g_mix[0][None], w_in_b, cos_t, sin_t, qg, kg, gv, bsum, ws2,
                         bs_t, tm_pre)
    kct, vc = _ctx_kv(ctx, mods[0], g_mix[0][None], w_kv_b, kg, bsum, b)
    attn = _attention(q, kt, v, kct, vc, qg, kg, tq)
    x1 = _post0(x, attn, gm, mods[0], g_ffn[0][None], woa, wog, w1b, w3b, w2b, 0, tm_ffn)

    band = _band_matrices(POOL_SUB)
    return _layer1(x1, mods[1], g_mix[1][None], g_ffn[1][None], pool_scale[0][None], band,
                   w_pool[0].astype(BF16), w1b, w3b, w2b, g_final[None], 1, tm_ffn)
```

```python
import functools

import numpy as np
import jax
import jax.numpy as jnp
from jax import lax
from jax.experimental import pallas as pl
from jax.experimental.pallas import tpu as pltpu

D_MODEL = 1024
GRID_W = 64
N_HEADS = 8
N_KV_HEADS = 2
HEAD_DIM = 64
GQA_GROUP = N_HEADS // N_KV_HEADS
ATTN_WIDTH = N_HEADS * HEAD_DIM
KV_WIDTH = N_KV_HEADS * HEAD_DIM
ROPE_THETA = 10000.0
GMLP_GROUPS = 8
GMLP_GROUP_DIM = 64
GMLP_WIDTH = GMLP_GROUPS * GMLP_GROUP_DIM
CHUNK = 128
POOL_WINDOWS = (2, 4, 8, 16)
POOL_GROUP_DIM = D_MODEL // len(POOL_WINDOWS)
EPS = 1e-6
Q_SCALE = float(HEAD_DIM ** -0.5 * np.log2(np.e))

LANES = 128
HALO = 8
COND_ROWS = 16
VMEM_LIMIT = 56 * 1024 * 1024
POOL_SUB = 64
CTX_BATCHES_PER_STEP = 4
KEY_TILE = 256
MAX_SAFE_SHIFT = 60.0

F32 = jnp.float32
BF16 = jnp.bfloat16


def _const_spec(shape):
    nd = len(shape)
    return pl.BlockSpec(shape, lambda *_: (0,) * nd, pipeline_mode=pl.Buffered(1))


def _layer_spec(shape, layer):
    nd = len(shape)
    return pl.BlockSpec((None,) + tuple(shape), lambda *_: (layer,) + (0,) * nd,
                        pipeline_mode=pl.Buffered(1))


def _mod_spec(d, layer, row):
    return pl.BlockSpec((None, 1, 6, d), lambda *idx: (layer, row(*idx), 0, 0))


def _params(n_axes):
    return pltpu.CompilerParams(dimension_semantics=("arbitrary",) * n_axes,
                                vmem_limit_bytes=VMEM_LIMIT)


def _rms_rows(x, gain):
    ms = jnp.mean(x * x, axis=-1, keepdims=True)
    return (x * lax.rsqrt(ms + EPS)) * gain


def _norm_modulate(x, gain, shift, scale):
    return _rms_rows(x, gain * (1.0 + scale)) + shift


def _seg_mean_sq(ta, tb, bsum_ref):
    sq = jnp.concatenate([(ta * ta).astype(BF16), (tb * tb).astype(BF16)], axis=1)
    ss = jnp.dot(sq, bsum_ref[...], preferred_element_type=F32) * (1.0 / HEAD_DIM)
    return ss[:, :LANES], ss[:, LANES:]


def _swiglu(h, w1_ref, w3_ref, w2_ref):
    a = jnp.dot(h, w1_ref[...], preferred_element_type=F32)
    b = jnp.dot(h, w3_ref[...], preferred_element_type=F32)
    g = (a * jax.nn.sigmoid(a) * b).astype(BF16)
    return jnp.dot(g, w2_ref[...], preferred_element_type=F32)


def _adaln_kernel(cond_ref, w_ref, b_ref, o_ref):
    s = cond_ref[...]
    s = (s * jax.nn.sigmoid(s)).astype(BF16)
    o_ref[0] = jnp.dot(s, w_ref[0].astype(BF16), preferred_element_type=F32) + b_ref[0]


def _adaln(cond, w_ada, b_ada, tn=1536):
    depth, d, n = w_ada.shape
    return pl.pallas_call(
        _adaln_kernel,
        grid=(depth, n // tn),
        in_specs=[
            pl.BlockSpec((COND_ROWS, d), lambda l, j: (0, 0)),
            pl.BlockSpec((1, d, tn), lambda l, j: (l, 0, j)),
            pl.BlockSpec((1, 1, tn), lambda l, j: (l, 0, j)),
        ],
        out_specs=pl.BlockSpec((1, COND_ROWS, tn), lambda l, j: (l, 0, j)),
        out_shape=jax.ShapeDtypeStruct((depth, COND_ROWS, n), F32),
        compiler_params=_params(2),
        name="adaln",
    )(cond, w_ada, b_ada.reshape(depth, 1, n))


def _rope(t, cos, sin_signed, even_lane):
    partner = jnp.where(even_lane, pltpu.roll(t, LANES - 1, 1), pltpu.roll(t, 1, 1))
    return t * cos + partner * sin_signed


def _gelu(x):
    return 0.5 * x * (1.0 + lax.erf(x * np.float32(np.sqrt(0.5))))


def _store_values_with_ones(v, ve_ref, j=0):
    ve_ref[j, :, :LANES] = v.astype(BF16)
    ve_ref[j, :, LANES:] = jnp.ones(v.shape, BF16)


def _pre0_kernel(x_ref, mod_ref, gmix_ref, win_ref, cos_ref, sin_ref, qg_ref, kg_ref, gv_ref,
                 bsum_ref, ws_ref, bs_ref, q_ref, kt_ref, v_ref, gm_ref):
    tm = x_ref.shape[1]
    m = mod_ref[0]
    h = _norm_modulate(x_ref[0], gmix_ref[...], m[0:1], m[1:2])
    proj = jnp.dot(h.astype(BF16), win_ref[...], preferred_element_type=F32)

    lane = lax.broadcasted_iota(jnp.int32, (tm, LANES), 1)
    even_lane = (lane % 2) == 0
    cos = cos_ref[...]
    sin_signed = sin_ref[...]

    def head_norm_rope(t, mean_sq, gain):
        return _rope((t * lax.rsqrt(mean_sq + EPS)) * gain, cos, sin_signed, even_lane)

    u0 = ATTN_WIDTH + 2 * KV_WIDTH
    g0 = u0 + GMLP_WIDTH
    n_gm = GMLP_WIDTH // LANES
    blocks = [proj[:, g * LANES:(g + 1) * LANES] for g in range(GQA_GROUP)]
    blocks.append(proj[:, ATTN_WIDTH:ATTN_WIDTH + KV_WIDTH])
    blocks += [_gelu(proj[:, g0 + j * LANES:g0 + (j + 1) * LANES]) for j in range(n_gm)]
    mean_sq = []
    for a in range(0, len(blocks), 2):
        pair = _seg_mean_sq(blocks[a], blocks[min(a + 1, len(blocks) - 1)], bsum_ref)
        mean_sq += list(pair)

    for g in range(GQA_GROUP):
        q_ref[0, :, g * LANES:(g + 1) * LANES] = (
            head_norm_rope(blocks[g], mean_sq[g], qg_ref[...]) * Q_SCALE).astype(BF16)
    k = head_norm_rope(blocks[GQA_GROUP], mean_sq[GQA_GROUP], kg_ref[...])
    kt_ref[0] = k.T.astype(BF16)
    _store_values_with_ones(proj[:, ATTN_WIDTH + KV_WIDTH:ATTN_WIDTH + 2 * KV_WIDTH], v_ref)

    left = lax.broadcasted_iota(jnp.int32, (CHUNK, LANES), 1) < GMLP_GROUP_DIM
    for j in range(n_gm):
        u = _gelu(proj[:, u0 + j * LANES:u0 + (j + 1) * LANES])
        vv = blocks[GQA_GROUP + 1 + j]
        vg = (vv * lax.rsqrt(mean_sq[GQA_GROUP + 1 + j] + EPS)) * gv_ref[:, j * LANES:(j + 1) * LANES]
        bias = bs_ref[:, j * LANES:(j + 1) * LANES]
        for n in range(tm // CHUNK):
            blk = vg[n * CHUNK:(n + 1) * CHUNK]
            rhs = jnp.concatenate([jnp.where(left, blk, 0.0), jnp.where(left, 0.0, blk)],
                                  axis=0).astype(BF16)
            mixed = jnp.dot(ws_ref[j], rhs, preferred_element_type=F32) + bias
            gm_ref[0, n * CHUNK:(n + 1) * CHUNK, j * LANES:(j + 1) * LANES] = (
                u[n * CHUNK:(n + 1) * CHUNK] * mixed).astype(BF16)


def _pre0(x, mods, g_mix0, w_in_b, cos_t, sin_t, qg, kg, gv, bsum, ws2, bs_t, tm):
    b, s, d = x.shape
    nw = w_in_b.shape[1]
    return pl.pallas_call(
        _pre0_kernel,
        grid=(b, s // tm),
        in_specs=[
            pl.BlockSpec((1, tm, d), lambda bi, i: (bi, i, 0)),
            _mod_spec(d, 0, lambda bi, i: bi),
            _layer_spec((1, d), 0),
            _const_spec((d, nw)),
            pl.BlockSpec((tm, LANES), lambda bi, i: (i, 0)),
            pl.BlockSpec((tm, LANES), lambda bi, i: (i, 0)),
            _const_spec((1, LANES)),
            _const_spec((1, LANES)),
            _const_spec((1, GMLP_WIDTH)),
            _const_spec((2 * LANES, 2 * LANES)),
            _const_spec(ws2.shape),
            _const_spec(bs_t.shape),
        ],
        out_specs=[
            pl.BlockSpec((1, tm, ATTN_WIDTH), lambda bi, i: (bi, i, 0)),
            pl.BlockSpec((1, KV_WIDTH, tm), lambda bi, i: (bi, 0, i)),
            pl.BlockSpec((1, tm, 2 * LANES), lambda bi, i: (bi, i, 0)),
            pl.BlockSpec((1, tm, GMLP_WIDTH), lambda bi, i: (bi, i, 0)),
        ],
        out_shape=[
            jax.ShapeDtypeStruct((b, s, ATTN_WIDTH), BF16),
            jax.ShapeDtypeStruct((b, KV_WIDTH, s), BF16),
            jax.ShapeDtypeStruct((b, s, 2 * LANES), BF16),
            jax.ShapeDtypeStruct((b, s, GMLP_WIDTH), BF16),
        ],
        compiler_params=_params(2),
        name="pre0",
    )(x, mods, g_mix0, w_in_b, cos_t, sin_t, qg, kg, gv, bsum, ws2, bs_t)


def _ctx_kernel(c_ref, mod_ref, gmix_ref, wkv_ref, kg_ref, bsum_ref, kt_ref, v_ref):
    bb, n, d = c_ref.shape
    m = mod_ref[0]
    h = _norm_modulate(c_ref[...].reshape(bb * n, d), gmix_ref[...], m[0:1], m[1:2])
    proj = jnp.dot(h.astype(BF16), wkv_ref[...], preferred_element_type=F32)
    k = proj[:, :KV_WIDTH]
    k = (k * lax.rsqrt(_seg_mean_sq(k, k, bsum_ref)[0] + EPS)) * kg_ref[...]
    for j in range(bb):
        kt_ref[j] = k[j * n:(j + 1) * n].T.astype(BF16)
        _store_values_with_ones(proj[j * n:(j + 1) * n, KV_WIDTH:], v_ref, j)


def _ctx_kv(ctx, mods, g_mix0, w_kv_b, kg, bsum, ctx_row, bb):
    b, n, d = ctx.shape
    assert b % bb == 0
    return pl.pallas_call(
        _ctx_kernel,
        grid=(b // bb,),
        in_specs=[
            pl.BlockSpec((bb, n, d), lambda bi: (bi, 0, 0)),
            _mod_spec(d, 0, lambda bi: ctx_row),
            _layer_spec((1, d), 0),
            _const_spec((d, 2 * KV_WIDTH)),
            _const_spec((1, LANES)),
            _const_spec((2 * LANES, 2 * LANES)),
        ],
        out_specs=[
            pl.BlockSpec((bb, KV_WIDTH, n), lambda bi: (bi, 0, 0)),
            pl.BlockSpec((bb, n, 2 * LANES), lambda bi: (bi, 0, 0)),
        ],
        out_shape=[
            jax.ShapeDtypeStruct((b, KV_WIDTH, n), BF16),
            jax.ShapeDtypeStruct((b, n, 2 * LANES), BF16),
        ],
        compiler_params=_params(1),
        name="ctx_kv",
    )(ctx, mods, g_mix0, w_kv_b, kg, bsum)


def _attn_kernel(q_ref, kt_ref, ve_ref, kct_ref, vce_ref, qg_ref, kg_ref, o_ref, qz_ref, r_ref):
    tq = q_ref.shape[1]
    n_heads = GQA_GROUP * N_KV_HEADS
    left = lax.broadcasted_iota(jnp.int32, (tq, LANES), 1) < HEAD_DIM
    m = (Q_SCALE * HEAD_DIM) * jnp.max(jnp.abs(qg_ref[...])) * jnp.max(jnp.abs(kg_ref[...]))

    def masked_queries(g):
        qc = q_ref[0, :, g * LANES:(g + 1) * LANES].astype(F32)
        return jnp.where(left, qc, 0.0).astype(BF16), jnp.where(left, 0.0, qc).astype(BF16)

    def normalised(r):
        return r[:, :LANES] / r[:, LANES:]

    def store_group(g, o_kh0, o_kh1):
        o_ref[0, :, g * LANES:(g + 1) * LANES] = jnp.where(left, o_kh0, o_kh1).astype(BF16)

    def key_tiles():
        for k_ref, v_ref in ((kt_ref, ve_ref), (kct_ref, vce_ref)):
            for j in range(k_ref.shape[2] // KEY_TILE):
                yield (k_ref.at[0, :, j * KEY_TILE:(j + 1) * KEY_TILE],
                       v_ref.at[0, j * KEY_TILE:(j + 1) * KEY_TILE, :])

    def streamed():
        for g in range(GQA_GROUP):
            outs = []
            for qz in masked_queries(g):
                r = None
                for k_tile, v_tile in key_tiles():
                    s = jnp.dot(qz, k_tile[...], preferred_element_type=F32)
                    p = jnp.exp2(s - m).astype(BF16)
                    d = jnp.dot(p, v_tile[...], preferred_element_type=F32)
                    r = d if r is None else r + d
                outs.append(normalised(r))
            store_group(g, *outs)

    def exact_max():
        for g in range(GQA_GROUP):
            qz_ref[N_KV_HEADS * g], qz_ref[N_KV_HEADS * g + 1] = masked_queries(g)

        def head(u, carry):
            s1 = jnp.dot(qz_ref[u], kt_ref[0], preferred_element_type=F32)
            s2 = jnp.dot(qz_ref[u], kct_ref[0], preferred_element_type=F32)
            mx = jnp.maximum(jnp.max(s1, axis=-1, keepdims=True), jnp.max(s2, axis=-1, keepdims=True))
            r_ref[u] = normalised(
                jnp.dot(jnp.exp2(s1 - mx).astype(BF16), ve_ref[0], preferred_element_type=F32)
                + jnp.dot(jnp.exp2(s2 - mx).astype(BF16), vce_ref[0], preferred_element_type=F32))
            return carry
        lax.fori_loop(0, n_heads, head, 0)
        for g in range(GQA_GROUP):
            store_group(g, r_ref[N_KV_HEADS * g], r_ref[N_KV_HEADS * g + 1])

    lax.cond(m <= MAX_SAFE_SHIFT, streamed, exact_max)


def _attention(q, kt, ve, kct, vce, qg, kg, tq):
    b, s, _ = q.shape
    nc = kct.shape[2]
    n_heads = GQA_GROUP * N_KV_HEADS
    assert s % KEY_TILE == 0 and nc % KEY_TILE == 0
    return pl.pallas_call(
        _attn_kernel,
        grid=(b, s // tq),
        in_specs=[
            pl.BlockSpec((1, tq, ATTN_WIDTH), lambda bi, i: (bi, i, 0)),
            pl.BlockSpec((1, KV_WIDTH, s), lambda bi, i: (bi, 0, 0)),
            pl.BlockSpec((1, s, 2 * LANES), lambda bi, i: (bi, 0, 0)),
            pl.BlockSpec((1, KV_WIDTH, nc), lambda bi, i: (bi, 0, 0)),
            pl.BlockSpec((1, nc, 2 * LANES), lambda bi, i: (bi, 0, 0)),
            _const_spec((1, LANES)),
            _const_spec((1, LANES)),
        ],
        out_specs=pl.BlockSpec((1, tq, ATTN_WIDTH), lambda bi, i: (bi, i, 0)),
        out_shape=jax.ShapeDtypeStruct((b, s, ATTN_WIDTH), BF16),
        scratch_shapes=[
            pltpu.VMEM((n_heads, tq, LANES), BF16),
            pltpu.VMEM((n_heads, tq, LANES), F32),
        ],
        compiler_params=_params(2),
        name="attention",
    )(q, kt, ve, kct, vce, qg, kg)


def _post0_kernel(x_ref, a_ref, gm_ref, mod_ref, gffn_ref, woa_ref, wog_ref, w1_ref, w3_ref, w2_ref,
                  o_ref):
    m = mod_ref[0]
    mix = (jnp.dot(a_ref[0], woa_ref[...], preferred_element_type=F32)
           + jnp.dot(gm_ref[0], wog_ref[...], preferred_element_type=F32))
    x1 = x_ref[0] + m[2:3] * mix
    h = _norm_modulate(x1, gffn_ref[...], m[3:4], m[4:5])
    o_ref[0] = x1 + m[5:6] * _swiglu(h.astype(BF16), w1_ref, w3_ref, w2_ref)


def _post0(x, attn, gm, mods, g_ffn0, woa, wog, w1, w3, w2, layer, tm):
    b, s, d = x.shape
    dff = w1.shape[2]
    return pl.pallas_call(
        _post0_kernel,
        grid=(b, s // tm),
        in_specs=[
            pl.BlockSpec((1, tm, d), lambda bi, i: (bi, i, 0)),
            pl.BlockSpec((1, tm, ATTN_WIDTH), lambda bi, i: (bi, i, 0)),
            pl.BlockSpec((1, tm, GMLP_WIDTH), lambda bi, i: (bi, i, 0)),
            _mod_spec(d, layer, lambda bi, i: bi),
            _layer_spec((1, d), layer),
            _const_spec((ATTN_WIDTH, d)),
            _const_spec((GMLP_WIDTH, d)),
            _layer_spec((d, dff), layer),
            _layer_spec((d, dff), layer),
            _layer_spec((dff, d), layer),
        ],
        out_specs=pl.BlockSpec((1, tm, d), lambda bi, i: (bi, i, 0)),
        out_shape=jax.ShapeDtypeStruct((b, s, d), F32),
        compiler_params=_params(2),
        name="post0",
    )(x, attn, gm, mods, g_ffn0, woa, wog, w1, w3, w2)


def _layer1_kernel(x_ref, prev_ref, next_ref, mod_ref, gmix_ref, gffn_ref, ps_ref, band_ref, wp_ref,
                   w1_ref, w3_ref, w2_ref, gfin_ref, o_ref, *, seq_len):
    tm = x_ref.shape[1]
    i = pl.program_id(1)
    m = mod_ref[0]

    def norm_mod(t):
        return _norm_modulate(t, gmix_ref[...], m[0:1], m[1:2])

    x = x_ref[0]
    xn = norm_mod(x)
    xp = jnp.where(i > 0, norm_mod(prev_ref[0]), 0.0)
    xq = jnp.where(i < pl.num_programs(1) - 1, norm_mod(next_ref[0]), 0.0)
    ext = jnp.concatenate([xp, xn, xq], axis=0)
    ext_hi = ext.astype(BF16)
    ext_lo = (ext - ext_hi.astype(F32)).astype(BF16)

    sub = band_ref.shape[1]
    pos = i * tm + lax.broadcasted_iota(jnp.int32, (tm, LANES), 0)
    ys = []
    for gi, w in enumerate(POOL_WINDOWS):
        left_w = w // 2
        right_w = w - 1 - left_w
        cnt = (jnp.minimum(pos + right_w + 1, seq_len) - jnp.maximum(pos - left_w, 0)).astype(F32)
        inv_cnt = jnp.concatenate([1.0 / cnt] * (POOL_GROUP_DIM // LANES), axis=1)
        sl = slice(gi * POOL_GROUP_DIM, (gi + 1) * POOL_GROUP_DIM)
        sums = []
        for r in range(tm // sub):
            rows = slice(r * sub, r * sub + sub + 2 * HALO)
            hi_lo = jnp.concatenate([ext_hi[rows, sl], ext_lo[rows, sl]], axis=0)
            sums.append(jnp.dot(band_ref[gi], hi_lo, preferred_element_type=F32))
        pooled = jnp.concatenate(sums, axis=0) * inv_cnt - xn[:, sl]
        ys.append(jnp.dot(pooled.astype(BF16), wp_ref[gi], preferred_element_type=F32))
    y = jnp.concatenate(ys, axis=1) * ps_ref[...]
    x1 = x + m[2:3] * y
    h = _norm_modulate(x1, gffn_ref[...], m[3:4], m[4:5])
    x2 = x1 + m[5:6] * _swiglu(h.astype(BF16), w1_ref, w3_ref, w2_ref)
    o_ref[0] = _rms_rows(x2, gfin_ref[...])


def _layer1(x, mods, g_mix1, g_ffn1, pool_scale, band, wp, w1, w3, w2, g_final, layer, tm):
    b, s, d = x.shape
    dff = w1.shape[2]
    per = tm // HALO
    last = s // HALO - 1
    return pl.pallas_call(
        functools.partial(_layer1_kernel, seq_len=s),
        grid=(b, s // tm),
        in_specs=[
            pl.BlockSpec((1, tm, d), lambda bi, i: (bi, i, 0)),
            pl.BlockSpec((1, HALO, d), lambda bi, i: (bi, jnp.maximum(i * per - 1, 0), 0)),
            pl.BlockSpec((1, HALO, d), lambda bi, i: (bi, jnp.minimum((i + 1) * per, last), 0)),
            _mod_spec(d, layer, lambda bi, i: bi),
            _layer_spec((1, d), layer),
            _layer_spec((1, d), layer),
            _const_spec((1, d)),
            _const_spec(band.shape),
            _const_spec(wp.shape),
            _layer_spec((d, dff), layer),
            _layer_spec((d, dff), layer),
            _layer_spec((dff, d), layer),
            _const_spec((1, d)),
        ],
        out_specs=pl.BlockSpec((1, tm, d), lambda bi, i: (bi, i, 0)),
        out_shape=jax.ShapeDtypeStruct((b, s, d), F32),
        compiler_params=_params(2),
        name="layer1",
    )(x, x, x, mods, g_mix1, g_ffn1, pool_scale, band, wp, w1, w3, w2, g_final)


def _rope_tables(n):
    rows = n // GRID_W
    row = np.repeat(np.arange(rows), GRID_W).astype(np.float64)
    col = np.tile(np.arange(GRID_W), rows).astype(np.float64)
    half = HEAD_DIM // 2
    freqs = ROPE_THETA ** (-np.arange(0, half, 2, dtype=np.float64) / half)
    ang = np.concatenate([row[:, None] * freqs, col[:, None] * freqs], axis=-1)
    cos = np.tile(np.repeat(np.cos(ang), 2, axis=1), (1, LANES // HEAD_DIM))
    sin = np.tile(np.repeat(np.sin(ang), 2, axis=1), (1, LANES // HEAD_DIM))
    sign = np.where(np.arange(LANES) % 2 == 0, -1.0, 1.0)
    return jnp.asarray(cos, dtype=F32), jnp.asarray(sin * sign, dtype=F32)


def _band_matrices(sub):
    t = np.arange(sub)[:, None]
    e = np.arange(sub + 2 * HALO)[None, :]
    mats = []
    for w in POOL_WINDOWS:
        left = w // 2
        right = w - 1 - left
        member = ((e >= t + HALO - left) & (e <= t + HALO + right)).astype(np.float32)
        mats.append(np.concatenate([member, member], axis=1))
    return jnp.asarray(np.stack(mats), dtype=BF16)


def kernel(x, c, ctx, c_ctx, w_ada, b_ada, g_mix, g_ffn, w_in, w_out, q_norm, k_norm, gmlp_norm,
           w_spatial, b_spatial, w_pool, pool_scale, w1, w3, w2, g_final):
    b, s, d = x.shape
    depth = w_ada.shape[0]
    assert depth == 2 and d == D_MODEL and s % CHUNK == 0
    tm_pre, tq, tm_ffn = 1024, 512, 512

    cond = jnp.concatenate([c, c_ctx[None], jnp.zeros((COND_ROWS - b - 1, d), F32)], axis=0)
    mods = _adaln(cond, w_ada, b_ada).reshape(depth, COND_ROWS, 6, d)

    wi = w_in[0]
    wq = wi[:, :ATTN_WIDTH].reshape(d, N_KV_HEADS, GQA_GROUP, HEAD_DIM).transpose(0, 2, 1, 3)
    w_in_b = jnp.concatenate([wq.reshape(d, ATTN_WIDTH), wi[:, ATTN_WIDTH:]], axis=1).astype(BF16)
    w_kv_b = wi[:, ATTN_WIDTH:ATTN_WIDTH + 2 * KV_WIDTH].astype(BF16)
    wo = w_out[0]
    woa = wo[:ATTN_WIDTH].reshape(N_KV_HEADS, GQA_GROUP, HEAD_DIM, d).transpose(1, 0, 2, 3)
    woa = woa.reshape(ATTN_WIDTH, d).astype(BF16)
    wog = wo[ATTN_WIDTH:].astype(BF16)
    cos_t, sin_t = _rope_tables(s)
    qg = jnp.tile(q_norm[0], LANES // HEAD_DIM)[None]
    kg = jnp.tile(k_norm[0], LANES // HEAD_DIM)[None]
    gv = gmlp_norm[0].reshape(1, GMLP_WIDTH)
    seg = np.arange(2 * LANES) // HEAD_DIM
    bsum = jnp.asarray((seg[:, None] == seg[None, :]).astype(np.float32), dtype=BF16)
    ws = w_spatial[0]
    ws2 = jnp.concatenate([ws[0::2], ws[1::2]], axis=2).astype(BF16)
    bs_t = jnp.repeat(b_spatial[0].T, GMLP_GROUP_DIM, axis=1)
    w1b, w3b, w2b = w1.astype(BF16), w3.astype(BF16), w2.astype(BF16)

    g_mix3, g_ffn3 = g_mix.reshape(depth, 1, d), g_ffn.reshape(depth, 1, d)

    q, kt, v, gm = _pre0(x, mods, g_mix3, w_in_b, cos_t, sin_t, qg, kg, gv, bsum, ws2, bs_t, tm_pre)
    kct, vc = _ctx_kv(ctx, mods, g_mix3, w_kv_b, kg, bsum, b, CTX_BATCHES_PER_STEP)
    attn = _attention(q, kt, v, kct, vc, qg, kg, tq)
    x1 = _post0(x, attn, gm, mods, g_ffn3, woa, wog, w1b, w3b, w2b, 0, tm_ffn)

    band = _band_matrices(POOL_SUB)
    return _layer1(x1, mods, g_mix3, g_ffn3, pool_scale[:1], band,
                   w_pool[0].astype(BF16), w1b, w3b, w2b, g_final[None], 1, tm_ffn)
```

```python
import functools

import numpy as np
import jax
import jax.numpy as jnp
from jax import lax
from jax.experimental import pallas as pl
from jax.experimental.pallas import tpu as pltpu

D_MODEL = 1024
GRID_W = 64
N_HEADS = 8
N_KV_HEADS = 2
HEAD_DIM = 64
GQA_GROUP = N_HEADS // N_KV_HEADS
ATTN_WIDTH = N_HEADS * HEAD_DIM
KV_WIDTH = N_KV_HEADS * HEAD_DIM
ROPE_THETA = 10000.0
GMLP_GROUPS = 8
GMLP_GROUP_DIM = 64
GMLP_WIDTH = GMLP_GROUPS * GMLP_GROUP_DIM
CHUNK = 128
POOL_WINDOWS = (2, 4, 8, 16)
POOL_GROUP_DIM = D_MODEL // len(POOL_WINDOWS)
EPS = 1e-6
Q_SCALE = float(HEAD_DIM ** -0.5 * np.log2(np.e))

LANES = 128
HALO = 8
COND_ROWS = 16
VMEM_LIMIT = 56 * 1024 * 1024
POOL_SUB = 64
CTX_BATCHES_PER_STEP = 4
ROW_PASS = 512
KEY_TILE = 256
MAX_SAFE_SHIFT = 60.0

F32 = jnp.float32
BF16 = jnp.bfloat16


def _const_spec(shape):
    nd = len(shape)
    return pl.BlockSpec(shape, lambda *_: (0,) * nd, pipeline_mode=pl.Buffered(1))


def _layer_spec(shape, layer):
    nd = len(shape)
    return pl.BlockSpec((None,) + tuple(shape), lambda *_: (layer,) + (0,) * nd,
                        pipeline_mode=pl.Buffered(1))


def _mod_spec(d, layer, row):
    return pl.BlockSpec((None, 1, 6, d), lambda *idx: (layer, row(*idx), 0, 0))


def _params(n_axes):
    return pltpu.CompilerParams(dimension_semantics=("arbitrary",) * n_axes,
                                vmem_limit_bytes=VMEM_LIMIT)


def _rms_rows(x, gain):
    ms = jnp.mean(x * x, axis=-1, keepdims=True)
    return (x * lax.rsqrt(ms + EPS)) * gain


def _norm_modulate(x, gain, shift, scale):
    return _rms_rows(x, gain * (1.0 + scale)) + shift


def _seg_mean_sq(ta, tb, bsum_ref):
    sq = jnp.concatenate([(ta * ta).astype(BF16), (tb * tb).astype(BF16)], axis=1)
    ss = jnp.dot(sq, bsum_ref[...], preferred_element_type=F32) * (1.0 / HEAD_DIM)
    return ss[:, :LANES], ss[:, LANES:]


def _swiglu(h, w1_ref, w3_ref, w2_ref):
    a = jnp.dot(h, w1_ref[...], preferred_element_type=F32)
    b = jnp.dot(h, w3_ref[...], preferred_element_type=F32)
    g = (a * jax.nn.sigmoid(a) * b).astype(BF16)
    return jnp.dot(g, w2_ref[...], preferred_element_type=F32)


def _adaln_kernel(cond_ref, w_ref, b_ref, o_ref):
    s = cond_ref[...]
    s = (s * jax.nn.sigmoid(s)).astype(BF16)
    o_ref[0] = jnp.dot(s, w_ref[0].astype(BF16), preferred_element_type=F32) + b_ref[0]


def _adaln(cond, w_ada, b_ada, tn=1536):
    depth, d, n = w_ada.shape
    return pl.pallas_call(
        _adaln_kernel,
        grid=(depth, n // tn),
        in_specs=[
            pl.BlockSpec((COND_ROWS, d), lambda l, j: (0, 0)),
            pl.BlockSpec((1, d, tn), lambda l, j: (l, 0, j)),
            pl.BlockSpec((1, 1, tn), lambda l, j: (l, 0, j)),
        ],
        out_specs=pl.BlockSpec((1, COND_ROWS, tn), lambda l, j: (l, 0, j)),
        out_shape=jax.ShapeDtypeStruct((depth, COND_ROWS, n), F32),
        compiler_params=_params(2),
        name="adaln",
    )(cond, w_ada, b_ada.reshape(depth, 1, n))


def _rope(t, cos, sin_signed, even_lane):
    partner = jnp.where(even_lane, pltpu.roll(t, LANES - 1, 1), pltpu.roll(t, 1, 1))
    return t * cos + partner * sin_signed


def _gelu(x):
    return 0.5 * x * (1.0 + lax.erf(x * np.float32(np.sqrt(0.5))))


def _store_values_with_ones(v, ve_ref, j=0):
    ve_ref[j, :, :LANES] = v.astype(BF16)
    ve_ref[j, :, LANES:] = jnp.ones(v.shape, BF16)


def _pre0_kernel(x_ref, mod_ref, gmix_ref, win_ref, cos_ref, sin_ref, qg_ref, kg_ref, gv_ref,
                 bsum_ref, ws_ref, bs_ref, q_ref, kt_ref, v_ref, gm_ref):
    tm = x_ref.shape[1]
    m = mod_ref[0]
    h = _norm_modulate(x_ref[0], gmix_ref[...], m[0:1], m[1:2])
    proj = jnp.dot(h.astype(BF16), win_ref[...], preferred_element_type=F32)

    lane = lax.broadcasted_iota(jnp.int32, (tm, LANES), 1)
    even_lane = (lane % 2) == 0
    cos = cos_ref[...]
    sin_signed = sin_ref[...]

    def head_norm_rope(t, mean_sq, gain):
        return _rope((t * lax.rsqrt(mean_sq + EPS)) * gain, cos, sin_signed, even_lane)

    u0 = ATTN_WIDTH + 2 * KV_WIDTH
    g0 = u0 + GMLP_WIDTH
    n_gm = GMLP_WIDTH // LANES
    blocks = [proj[:, g * LANES:(g + 1) * LANES] for g in range(GQA_GROUP)]
    blocks.append(proj[:, ATTN_WIDTH:ATTN_WIDTH + KV_WIDTH])
    blocks += [_gelu(proj[:, g0 + j * LANES:g0 + (j + 1) * LANES]) for j in range(n_gm)]
    mean_sq = []
    for a in range(0, len(blocks), 2):
        pair = _seg_mean_sq(blocks[a], blocks[min(a + 1, len(blocks) - 1)], bsum_ref)
        mean_sq += list(pair)

    for g in range(GQA_GROUP):
        q_ref[0, :, g * LANES:(g + 1) * LANES] = (
            head_norm_rope(blocks[g], mean_sq[g], qg_ref[...]) * Q_SCALE).astype(BF16)
    k = head_norm_rope(blocks[GQA_GROUP], mean_sq[GQA_GROUP], kg_ref[...])
    kt_ref[0] = k.T.astype(BF16)
    _store_values_with_ones(proj[:, ATTN_WIDTH + KV_WIDTH:ATTN_WIDTH + 2 * KV_WIDTH], v_ref)

    left = lax.broadcasted_iota(jnp.int32, (CHUNK, LANES), 1) < GMLP_GROUP_DIM
    for j in range(n_gm):
        u = _gelu(proj[:, u0 + j * LANES:u0 + (j + 1) * LANES])
        vv = blocks[GQA_GROUP + 1 + j]
        vg = (vv * lax.rsqrt(mean_sq[GQA_GROUP + 1 + j] + EPS)) * gv_ref[:, j * LANES:(j + 1) * LANES]
        bias = bs_ref[:, j * LANES:(j + 1) * LANES]
        for n in range(tm // CHUNK):
            blk = vg[n * CHUNK:(n + 1) * CHUNK]
            rhs = jnp.concatenate([jnp.where(left, blk, 0.0), jnp.where(left, 0.0, blk)],
                                  axis=0).astype(BF16)
            mixed = jnp.dot(ws_ref[j], rhs, preferred_element_type=F32) + bias
            gm_ref[0, n * CHUNK:(n + 1) * CHUNK, j * LANES:(j + 1) * LANES] = (
                u[n * CHUNK:(n + 1) * CHUNK] * mixed).astype(BF16)


def _pre0(x, mods, g_mix0, w_in_b, cos_t, sin_t, qg, kg, gv, bsum, ws2, bs_t, tm):
    b, s, d = x.shape
    nw = w_in_b.shape[1]
    return pl.pallas_call(
        _pre0_kernel,
        grid=(b, s // tm),
        in_specs=[
            pl.BlockSpec((1, tm, d), lambda bi, i: (bi, i, 0)),
            _mod_spec(d, 0, lambda bi, i: bi),
            _layer_spec((1, d), 0),
            _const_spec((d, nw)),
            pl.BlockSpec((tm, LANES), lambda bi, i: (i, 0)),
            pl.BlockSpec((tm, LANES), lambda bi, i: (i, 0)),
            _const_spec((1, LANES)),
            _const_spec((1, LANES)),
            _const_spec((1, GMLP_WIDTH)),
            _const_spec((2 * LANES, 2 * LANES)),
            _const_spec(ws2.shape),
            _const_spec(bs_t.shape),
        ],
        out_specs=[
            pl.BlockSpec((1, tm, ATTN_WIDTH), lambda bi, i: (bi, i, 0)),
            pl.BlockSpec((1, KV_WIDTH, tm), lambda bi, i: (bi, 0, i)),
            pl.BlockSpec((1, tm, 2 * LANES), lambda bi, i: (bi, i, 0)),
            pl.BlockSpec((1, tm, GMLP_WIDTH), lambda bi, i: (bi, i, 0)),
        ],
        out_shape=[
            jax.ShapeDtypeStruct((b, s, ATTN_WIDTH), BF16),
            jax.ShapeDtypeStruct((b, KV_WIDTH, s), BF16),
            jax.ShapeDtypeStruct((b, s, 2 * LANES), BF16),
            jax.ShapeDtypeStruct((b, s, GMLP_WIDTH), BF16),
        ],
        compiler_params=_params(2),
        name="pre0",
    )(x, mods, g_mix0, w_in_b, cos_t, sin_t, qg, kg, gv, bsum, ws2, bs_t)


def _ctx_kernel(c_ref, mod_ref, gmix_ref, wkv_ref, kg_ref, bsum_ref, kt_ref, v_ref):
    bb, n, d = c_ref.shape
    m = mod_ref[0]
    h = _norm_modulate(c_ref[...].reshape(bb * n, d), gmix_ref[...], m[0:1], m[1:2])
    proj = jnp.dot(h.astype(BF16), wkv_ref[...], preferred_element_type=F32)
    k = proj[:, :KV_WIDTH]
    k = (k * lax.rsqrt(_seg_mean_sq(k, k, bsum_ref)[0] + EPS)) * kg_ref[...]
    for j in range(bb):
        kt_ref[j] = k[j * n:(j + 1) * n].T.astype(BF16)
        _store_values_with_ones(proj[j * n:(j + 1) * n, KV_WIDTH:], v_ref, j)


def _ctx_kv(ctx, mods, g_mix0, w_kv_b, kg, bsum, ctx_row, bb):
    b, n, d = ctx.shape
    assert b % bb == 0
    return pl.pallas_call(
        _ctx_kernel,
        grid=(b // bb,),
        in_specs=[
            pl.BlockSpec((bb, n, d), lambda bi: (bi, 0, 0)),
            _mod_spec(d, 0, lambda bi: ctx_row),
            _layer_spec((1, d), 0),
            _const_spec((d, 2 * KV_WIDTH)),
            _const_spec((1, LANES)),
            _const_spec((2 * LANES, 2 * LANES)),
        ],
        out_specs=[
            pl.BlockSpec((bb, KV_WIDTH, n), lambda bi: (bi, 0, 0)),
            pl.BlockSpec((bb, n, 2 * LANES), lambda bi: (bi, 0, 0)),
        ],
        out_shape=[
            jax.ShapeDtypeStruct((b, KV_WIDTH, n), BF16),
            jax.ShapeDtypeStruct((b, n, 2 * LANES), BF16),
        ],
        compiler_params=_params(1),
        name="ctx_kv",
    )(ctx, mods, g_mix0, w_kv_b, kg, bsum)


def _attn_kernel(q_ref, kt_ref, ve_ref, kct_ref, vce_ref, qg_ref, kg_ref, o_ref, qz_ref, r_ref):
    tq = q_ref.shape[1]
    n_heads = GQA_GROUP * N_KV_HEADS
    left = lax.broadcasted_iota(jnp.int32, (ROW_PASS, LANES), 1) < HEAD_DIM
    m = (Q_SCALE * HEAD_DIM) * jnp.max(jnp.abs(qg_ref[...])) * jnp.max(jnp.abs(kg_ref[...]))

    def masked_queries(g, r0):
        qc = q_ref[0, r0:r0 + ROW_PASS, g * LANES:(g + 1) * LANES].astype(F32)
        return jnp.where(left, qc, 0.0).astype(BF16), jnp.where(left, 0.0, qc).astype(BF16)

    def normalised(r):
        return r[:, :LANES] / r[:, LANES:]

    def store_group(g, r0, o_kh0, o_kh1):
        o_ref[0, r0:r0 + ROW_PASS, g * LANES:(g + 1) * LANES] = jnp.where(left, o_kh0, o_kh1).astype(BF16)

    def key_tiles():
        for k_ref, v_ref in ((kt_ref, ve_ref), (kct_ref, vce_ref)):
            for j in range(k_ref.shape[2] // KEY_TILE):
                yield (k_ref.at[0, :, j * KEY_TILE:(j + 1) * KEY_TILE],
                       v_ref.at[0, j * KEY_TILE:(j + 1) * KEY_TILE, :])

    def streamed():
        for r0 in range(0, tq, ROW_PASS):
            for g in range(GQA_GROUP):
                outs = []
                for qz in masked_queries(g, r0):
                    r = None
                    for k_tile, v_tile in key_tiles():
                        s = jnp.dot(qz, k_tile[...], preferred_element_type=F32)
                        p = jnp.exp2(s - m).astype(BF16)
                        d = jnp.dot(p, v_tile[...], preferred_element_type=F32)
                        r = d if r is None else r + d
                    outs.append(normalised(r))
                store_group(g, r0, *outs)

    def exact_max():
        for r0 in range(0, tq, ROW_PASS):
            exact_max_rows(r0)

    def exact_max_rows(r0):
        for g in range(GQA_GROUP):
            qz_ref[N_KV_HEADS * g], qz_ref[N_KV_HEADS * g + 1] = masked_queries(g, r0)

        def head(u, carry):
            s1 = jnp.dot(qz_ref[u], kt_ref[0], preferred_element_type=F32)
            s2 = jnp.dot(qz_ref[u], kct_ref[0], preferred_element_type=F32)
            mx = jnp.maximum(jnp.max(s1, axis=-1, keepdims=True), jnp.max(s2, axis=-1, keepdims=True))
            r_ref[u] = normalised(
                jnp.dot(jnp.exp2(s1 - mx).astype(BF16), ve_ref[0], preferred_element_type=F32)
                + jnp.dot(jnp.exp2(s2 - mx).astype(BF16), vce_ref[0], preferred_element_type=F32))
            return carry
        lax.fori_loop(0, n_heads, head, 0)
        for g in range(GQA_GROUP):
            store_group(g, r0, r_ref[N_KV_HEADS * g], r_ref[N_KV_HEADS * g + 1])

    lax.cond(m <= MAX_SAFE_SHIFT, streamed, exact_max)


def _attention(q, kt, ve, kct, vce, qg, kg, tq):
    b, s, _ = q.shape
    nc = kct.shape[2]
    n_heads = GQA_GROUP * N_KV_HEADS
    assert s % KEY_TILE == 0 and nc % KEY_TILE == 0 and tq % ROW_PASS == 0
    return pl.pallas_call(
        _attn_kernel,
        grid=(b, s // tq),
        in_specs=[
            pl.BlockSpec((1, tq, ATTN_WIDTH), lambda bi, i: (bi, i, 0)),
            pl.BlockSpec((1, KV_WIDTH, s), lambda bi, i: (bi, 0, 0)),
            pl.BlockSpec((1, s, 2 * LANES), lambda bi, i: (bi, 0, 0)),
            pl.BlockSpec((1, KV_WIDTH, nc), lambda bi, i: (bi, 0, 0)),
            pl.BlockSpec((1, nc, 2 * LANES), lambda bi, i: (bi, 0, 0)),
            _const_spec((1, LANES)),
            _const_spec((1, LANES)),
        ],
        out_specs=pl.BlockSpec((1, tq, ATTN_WIDTH), lambda bi, i: (bi, i, 0)),
        out_shape=jax.ShapeDtypeStruct((b, s, ATTN_WIDTH), BF16),
        scratch_shapes=[
            pltpu.VMEM((n_heads, ROW_PASS, LANES), BF16),
            pltpu.VMEM((n_heads, ROW_PASS, LANES), F32),
        ],
        compiler_params=_params(2),
        name="attention",
    )(q, kt, ve, kct, vce, qg, kg)


def _post0_kernel(x_ref, a_ref, gm_ref, mod_ref, gffn_ref, woa_ref, wog_ref, w1_ref, w3_ref, w2_ref,
                  o_ref):
    m = mod_ref[0]
    mix = (jnp.dot(a_ref[0], woa_ref[...], preferred_element_type=F32)
           + jnp.dot(gm_ref[0], wog_ref[...], preferred_element_type=F32))
    x1 = x_ref[0] + m[2:3] * mix
    h = _norm_modulate(x1, gffn_ref[...], m[3:4], m[4:5])
    o_ref[0] = x1 + m[5:6] * _swiglu(h.astype(BF16), w1_ref, w3_ref, w2_ref)


def _post0(x, attn, gm, mods, g_ffn0, woa, wog, w1, w3, w2, layer, tm):
    b, s, d = x.shape
    dff = w1.shape[2]
    return pl.pallas_call(
        _post0_kernel,
        grid=(b, s // tm),
        in_specs=[
            pl.BlockSpec((1, tm, d), lambda bi, i: (bi, i, 0)),
            pl.BlockSpec((1, tm, ATTN_WIDTH), lambda bi, i: (bi, i, 0)),
            pl.BlockSpec((1, tm, GMLP_WIDTH), lambda bi, i: (bi, i, 0)),
            _mod_spec(d, layer, lambda bi, i: bi),
            _layer_spec((1, d), layer),
            _const_spec((ATTN_WIDTH, d)),
            _const_spec((GMLP_WIDTH, d)),
            _layer_spec((d, dff), layer),
            _layer_spec((d, dff), layer),
            _layer_spec((dff, d), layer),
        ],
        out_specs=pl.BlockSpec((1, tm, d), lambda bi, i: (bi, i, 0)),
        out_shape=jax.ShapeDtypeStruct((b, s, d), F32),
        compiler_params=_params(2),
        name="post0",
    )(x, attn, gm, mods, g_ffn0, woa, wog, w1, w3, w2)


def _layer1_kernel(x_ref, prev_ref, next_ref, mod_ref, gmix_ref, gffn_ref, ps_ref, band_ref, wp_ref,
                   w1_ref, w3_ref, w2_ref, gfin_ref, o_ref, *, seq_len):
    tm = x_ref.shape[1]
    i = pl.program_id(1)
    m = mod_ref[0]

    def norm_mod(t):
        return _norm_modulate(t, gmix_ref[...], m[0:1], m[1:2])

    x = x_ref[0]
    xn = norm_mod(x)
    xp = jnp.where(i > 0, norm_mod(prev_ref[0]), 0.0)
    xq = jnp.where(i < pl.num_programs(1) - 1, norm_mod(next_ref[0]), 0.0)
    ext = jnp.concatenate([xp, xn, xq], axis=0)
    ext_hi = ext.astype(BF16)
    ext_lo = (ext - ext_hi.astype(F32)).astype(BF16)

    sub = band_ref.shape[1]
    pos = i * tm + lax.broadcasted_iota(jnp.int32, (tm, LANES), 0)
    ys = []
    for gi, w in enumerate(POOL_WINDOWS):
        left_w = w // 2
        right_w = w - 1 - left_w
        cnt = (jnp.minimum(pos + right_w + 1, seq_len) - jnp.maximum(pos - left_w, 0)).astype(F32)
        inv_cnt = jnp.concatenate([1.0 / cnt] * (POOL_GROUP_DIM // LANES), axis=1)
        sl = slice(gi * POOL_GROUP_DIM, (gi + 1) * POOL_GROUP_DIM)
        sums = []
        for r in range(tm // sub):
            rows = slice(r * sub, r * sub + sub + 2 * HALO)
            hi_lo = jnp.concatenate([ext_hi[rows, sl], ext_lo[rows, sl]], axis=0)
            sums.append(jnp.dot(band_ref[gi], hi_lo, preferred_element_type=F32))
        pooled = jnp.concatenate(sums, axis=0) * inv_cnt - xn[:, sl]
        ys.append(jnp.dot(pooled.astype(BF16), wp_ref[gi], preferred_element_type=F32))
    y = jnp.concatenate(ys, axis=1) * ps_ref[...]
    x1 = x + m[2:3] * y
    h = _norm_modulate(x1, gffn_ref[...], m[3:4], m[4:5])
    x2 = x1 + m[5:6] * _swiglu(h.astype(BF16), w1_ref, w3_ref, w2_ref)
    o_ref[0] = _rms_rows(x2, gfin_ref[...])


def _layer1(x, mods, g_mix1, g_ffn1, pool_scale, band, wp, w1, w3, w2, g_final, layer, tm):
    b, s, d = x.shape
    dff = w1.shape[2]
    per = tm // HALO
    last = s // HALO - 1
    return pl.pallas_call(
        functools.partial(_layer1_kernel, seq_len=s),
        grid=(b, s // tm),
        in_specs=[
            pl.BlockSpec((1, tm, d), lambda bi, i: (bi, i, 0)),
            pl.BlockSpec((1, HALO, d), lambda bi, i: (bi, jnp.maximum(i * per - 1, 0), 0)),
            pl.BlockSpec((1, HALO, d), lambda bi, i: (bi, jnp.minimum((i + 1) * per, last), 0)),
            _mod_spec(d, layer, lambda bi, i: bi),
            _layer_spec((1, d), layer),
            _layer_spec((1, d), layer),
            _const_spec((1, d)),
            _const_spec(band.shape),
            _const_spec(wp.shape),
            _layer_spec((d, dff), layer),
            _layer_spec((d, dff), layer),
            _layer_spec((dff, d), layer),
            _const_spec((1, d)),
        ],
        out_specs=pl.BlockSpec((1, tm, d), lambda bi, i: (bi, i, 0)),
        out_shape=jax.ShapeDtypeStruct((b, s, d), F32),
        compiler_params=_params(2),
        name="layer1",
    )(x, x, x, mods, g_mix1, g_ffn1, pool_scale, band, wp, w1, w3, w2, g_final)


def _rope_tables(n):
    rows = n // GRID_W
    row = np.repeat(np.arange(rows), GRID_W).astype(np.float64)
    col = np.tile(np.arange(GRID_W), rows).astype(np.float64)
    half = HEAD_DIM // 2
    freqs = ROPE_THETA ** (-np.arange(0, half, 2, dtype=np.float64) / half)
    ang = np.concatenate([row[:, None] * freqs, col[:, None] * freqs], axis=-1)
    cos = np.tile(np.repeat(np.cos(ang), 2, axis=1), (1, LANES // HEAD_DIM))
    sin = np.tile(np.repeat(np.sin(ang), 2, axis=1), (1, LANES // HEAD_DIM))
    sign = np.where(np.arange(LANES) % 2 == 0, -1.0, 1.0)
    return jnp.asarray(cos, dtype=F32), jnp.asarray(sin * sign, dtype=F32)


def _band_matrices(sub):
    t = np.arange(sub)[:, None]
    e = np.arange(sub + 2 * HALO)[None, :]
    mats = []
    for w in POOL_WINDOWS:
        left = w // 2
        right = w - 1 - left
        member = ((e >= t + HALO - left) & (e <= t + HALO + right)).astype(np.float32)
        mats.append(np.concatenate([member, member], axis=1))
    return jnp.asarray(np.stack(mats), dtype=BF16)


def kernel(x, c, ctx, c_ctx, w_ada, b_ada, g_mix, g_ffn, w_in, w_out, q_norm, k_norm, gmlp_norm,
           w_spatial, b_spatial, w_pool, pool_scale, w1, w3, w2, g_final):
    b, s, d = x.shape
    depth = w_ada.shape[0]
    assert depth == 2 and d == D_MODEL and s % CHUNK == 0
    tm_pre, tq, tm_ffn = 1024, 1024, 512

    cond = jnp.concatenate([c, c_ctx[None], jnp.zeros((COND_ROWS - b - 1, d), F32)], axis=0)
    mods = _adaln(cond, w_ada, b_ada).reshape(depth, COND_ROWS, 6, d)

    wi = w_in[0]
    wq = wi[:, :ATTN_WIDTH].reshape(d, N_KV_HEADS, GQA_GROUP, HEAD_DIM).transpose(0, 2, 1, 3)
    w_in_b = jnp.concatenate([wq.reshape(d, ATTN_WIDTH), wi[:, ATTN_WIDTH:]], axis=1).astype(BF16)
    w_kv_b = wi[:, ATTN_WIDTH:ATTN_WIDTH + 2 * KV_WIDTH].astype(BF16)
    wo = w_out[0]
    woa = wo[:ATTN_WIDTH].reshape(N_KV_HEADS, GQA_GROUP, HEAD_DIM, d).transpose(1, 0, 2, 3)
    woa = woa.reshape(ATTN_WIDTH, d).astype(BF16)
    wog = wo[ATTN_WIDTH:].astype(BF16)
    cos_t, sin_t = _rope_tables(s)
    qg = jnp.tile(q_norm[0], LANES // HEAD_DIM)[None]
    kg = jnp.tile(k_norm[0], LANES // HEAD_DIM)[None]
    gv = gmlp_norm[0].reshape(1, GMLP_WIDTH)
    seg = np.arange(2 * LANES) // HEAD_DIM
    bsum = jnp.asarray((seg[:, None] == seg[None, :]).astype(np.float32), dtype=BF16)
    ws = w_spatial[0]
    ws2 = jnp.concatenate([ws[0::2], ws[1::2]], axis=2).astype(BF16)
    bs_t = jnp.repeat(b_spatial[0].T, GMLP_GROUP_DIM, axis=1)
    w1b, w3b, w2b = w1.astype(BF16), w3.astype(BF16), w2.astype(BF16)

    g_mix3, g_ffn3 = g_mix.reshape(depth, 1, d), g_ffn.reshape(depth, 1, d)

    q, kt, v, gm = _pre0(x, mods, g_mix3, w_in_b, cos_t, sin_t, qg, kg, gv, bsum, ws2, bs_t, tm_pre)
    kct, vc = _ctx_kv(ctx, mods, g_mix3, w_kv_b, kg, bsum, b, CTX_BATCHES_PER_STEP)
    attn = _attention(q, kt, v, kct, vc, qg, kg, tq)
    x1 = _post0(x, attn, gm, mods, g_ffn3, woa, wog, w1b, w3b, w2b, 0, tm_ffn)

    band = _band_matrices(POOL_SUB)
    return _layer1(x1, mods, g_mix3, g_ffn3, pool_scale[:1], band,
                   w_pool[0].astype(BF16), w1b, w3b, w2b, g_final[None], 1, tm_ffn)
```

```python
import functools

import numpy as np
import jax
import jax.numpy as jnp
from jax import lax
from jax.experimental import pallas as pl
from jax.experimental.pallas import tpu as pltpu

D_MODEL = 1024
GRID_W = 64
N_HEADS = 8
N_KV_HEADS = 2
HEAD_DIM = 64
GQA_GROUP = N_HEADS // N_KV_HEADS
ATTN_WIDTH = N_HEADS * HEAD_DIM
KV_WIDTH = N_KV_HEADS * HEAD_DIM
ROPE_THETA = 10000.0
GMLP_GROUPS = 8
GMLP_GROUP_DIM = 64
GMLP_WIDTH = GMLP_GROUPS * GMLP_GROUP_DIM
CHUNK = 128
POOL_WINDOWS = (2, 4, 8, 16)
POOL_GROUP_DIM = D_MODEL // len(POOL_WINDOWS)
EPS = 1e-6
Q_SCALE = float(HEAD_DIM ** -0.5 * np.log2(np.e))

LANES = 128
HALO = 8
COND_ROWS = 16
VMEM_LIMIT = 56 * 1024 * 1024
POOL_SUB = 64
CTX_BATCHES_PER_STEP = 4
KEY_TILE = 256
MAX_SAFE_SHIFT = 60.0

F32 = jnp.float32
BF16 = jnp.bfloat16


def _const_spec(shape):
    nd = len(shape)
    return pl.BlockSpec(shape, lambda *_: (0,) * nd, pipeline_mode=pl.Buffered(1))


def _layer_spec(shape, layer):
    nd = len(shape)
    return pl.BlockSpec((None,) + tuple(shape), lambda *_: (layer,) + (0,) * nd,
                        pipeline_mode=pl.Buffered(1))


def _mod_spec(d, layer, row):
    return pl.BlockSpec((None, 1, 6, d), lambda *idx: (layer, row(*idx), 0, 0))


def _params(n_axes):
    return pltpu.CompilerParams(dimension_semantics=("arbitrary",) * n_axes,
                                vmem_limit_bytes=VMEM_LIMIT)


def _rms_rows(x, gain):
    ms = jnp.mean(x * x, axis=-1, keepdims=True)
    return (x * lax.rsqrt(ms + EPS)) * gain


def _norm_modulate(x, gain, shift, scale):
    return _rms_rows(x, gain * (1.0 + scale)) + shift


def _seg_mean_sq(ta, tb, bsum_ref):
    sq = jnp.concatenate([(ta * ta).astype(BF16), (tb * tb).astype(BF16)], axis=1)
    ss = jnp.dot(sq, bsum_ref[...], preferred_element_type=F32) * (1.0 / HEAD_DIM)
    return ss[:, :LANES], ss[:, LANES:]


def _swiglu(h, w1_ref, w3_ref, w2_ref):
    a = jnp.dot(h, w1_ref[...], preferred_element_type=F32)
    b = jnp.dot(h, w3_ref[...], preferred_element_type=F32)
    g = (a * jax.nn.sigmoid(a) * b).astype(BF16)
    return jnp.dot(g, w2_ref[...], preferred_element_type=F32)


def _adaln_kernel(cond_ref, w_ref, b_ref, o_ref):
    s = cond_ref[...]
    s = (s * jax.nn.sigmoid(s)).astype(BF16)
    o_ref[0] = jnp.dot(s, w_ref[0].astype(BF16), preferred_element_type=F32) + b_ref[0]


def _adaln(cond, w_ada, b_ada, tn=1536):
    depth, d, n = w_ada.shape
    return pl.pallas_call(
        _adaln_kernel,
        grid=(depth, n // tn),
        in_specs=[
            pl.BlockSpec((COND_ROWS, d), lambda l, j: (0, 0)),
            pl.BlockSpec((1, d, tn), lambda l, j: (l, 0, j)),
            pl.BlockSpec((1, 1, tn), lambda l, j: (l, 0, j)),
        ],
        out_specs=pl.BlockSpec((1, COND_ROWS, tn), lambda l, j: (l, 0, j)),
        out_shape=jax.ShapeDtypeStruct((depth, COND_ROWS, n), F32),
        compiler_params=_params(2),
        name="adaln",
    )(cond, w_ada, b_ada.reshape(depth, 1, n))


def _rope(t, cos, sin_signed, even_lane):
    partner = jnp.where(even_lane, pltpu.roll(t, LANES - 1, 1), pltpu.roll(t, 1, 1))
    return t * cos + partner * sin_signed


def _gelu(x):
    return 0.5 * x * (1.0 + lax.erf(x * np.float32(np.sqrt(0.5))))


def _store_values_with_ones(v, ve_ref, j=0):
    ve_ref[j, :, :LANES] = v.astype(BF16)
    ve_ref[j, :, LANES:] = jnp.ones(v.shape, BF16)


def _pre0_kernel(x_ref, mod_ref, gmix_ref, win_ref, cos_ref, sin_ref, qg_ref, kg_ref, gv_ref,
                 bsum_ref, ws_ref, bs_ref, q_ref, kt_ref, v_ref, gm_ref):
    tm = x_ref.shape[1]
    m = mod_ref[0]
    h = _norm_modulate(x_ref[0], gmix_ref[...], m[0:1], m[1:2])
    proj = jnp.dot(h.astype(BF16), win_ref[...], preferred_element_type=F32)

    lane = lax.broadcasted_iota(jnp.int32, (tm, LANES), 1)
    even_lane = (lane % 2) == 0
    cos = cos_ref[...]
    sin_signed = sin_ref[...]

    def head_norm_rope(t, mean_sq, gain):
        return _rope((t * lax.rsqrt(mean_sq + EPS)) * gain, cos, sin_signed, even_lane)

    u0 = ATTN_WIDTH + 2 * KV_WIDTH
    g0 = u0 + GMLP_WIDTH
    n_gm = GMLP_WIDTH // LANES
    blocks = [proj[:, g * LANES:(g + 1) * LANES] for g in range(GQA_GROUP)]
    blocks.append(proj[:, ATTN_WIDTH:ATTN_WIDTH + KV_WIDTH])
    blocks += [_gelu(proj[:, g0 + j * LANES:g0 + (j + 1) * LANES]) for j in range(n_gm)]
    mean_sq = []
    for a in range(0, len(blocks), 2):
        pair = _seg_mean_sq(blocks[a], blocks[min(a + 1, len(blocks) - 1)], bsum_ref)
        mean_sq += list(pair)

    for g in range(GQA_GROUP):
        q_ref[0, :, g * LANES:(g + 1) * LANES] = (
            head_norm_rope(blocks[g], mean_sq[g], qg_ref[...]) * Q_SCALE).astype(BF16)
    k = head_norm_rope(blocks[GQA_GROUP], mean_sq[GQA_GROUP], kg_ref[...])
    kt_ref[0] = k.T.astype(BF16)
    _store_values_with_ones(proj[:, ATTN_WIDTH + KV_WIDTH:ATTN_WIDTH + 2 * KV_WIDTH], v_ref)

    left = lax.broadcasted_iota(jnp.int32, (CHUNK, LANES), 1) < GMLP_GROUP_DIM
    for j in range(n_gm):
        u = _gelu(proj[:, u0 + j * LANES:u0 + (j + 1) * LANES])
        vv = blocks[GQA_GROUP + 1 + j]
        vg = (vv * lax.rsqrt(mean_sq[GQA_GROUP + 1 + j] + EPS)) * gv_ref[:, j * LANES:(j + 1) * LANES]
        bias = bs_ref[:, j * LANES:(j + 1) * LANES]
        for n in range(tm // CHUNK):
            blk = vg[n * CHUNK:(n + 1) * CHUNK]
            rhs = jnp.concatenate([jnp.where(left, blk, 0.0), jnp.where(left, 0.0, blk)],
                                  axis=0).astype(BF16)
            mixed = jnp.dot(ws_ref[j], rhs, preferred_element_type=F32) + bias
            gm_ref[0, n * CHUNK:(n + 1) * CHUNK, j * LANES:(j + 1) * LANES] = (
                u[n * CHUNK:(n + 1) * CHUNK] * mixed).astype(BF16)


def _pre0(x, mods, g_mix0, w_in_b, cos_t, sin_t, qg, kg, gv, bsum, ws2, bs_t, tm):
    b, s, d = x.shape
    nw = w_in_b.shape[1]
    return pl.pallas_call(
        _pre0_kernel,
        grid=(b, s // tm),
        in_specs=[
            pl.BlockSpec((1, tm, d), lambda bi, i: (bi, i, 0)),
            _mod_spec(d, 0, lambda bi, i: bi),
            _layer_spec((1, d), 0),
            _const_spec((d, nw)),
            pl.BlockSpec((tm, LANES), lambda bi, i: (i, 0)),
            pl.BlockSpec((tm, LANES), lambda bi, i: (i, 0)),
            _const_spec((1, LANES)),
            _const_spec((1, LANES)),
            _const_spec((1, GMLP_WIDTH)),
            _const_spec((2 * LANES, 2 * LANES)),
            _const_spec(ws2.shape),
            _const_spec(bs_t.shape),
        ],
        out_specs=[
            pl.BlockSpec((1, tm, ATTN_WIDTH), lambda bi, i: (bi, i, 0)),
            pl.BlockSpec((1, KV_WIDTH, tm), lambda bi, i: (bi, 0, i)),
            pl.BlockSpec((1, tm, 2 * LANES), lambda bi, i: (bi, i, 0)),
            pl.BlockSpec((1, tm, GMLP_WIDTH), lambda bi, i: (bi, i, 0)),
        ],
        out_shape=[
            jax.ShapeDtypeStruct((b, s, ATTN_WIDTH), BF16),
            jax.ShapeDtypeStruct((b, KV_WIDTH, s), BF16),
            jax.ShapeDtypeStruct((b, s, 2 * LANES), BF16),
            jax.ShapeDtypeStruct((b, s, GMLP_WIDTH), BF16),
        ],
        compiler_params=_params(2),
        name="pre0",
    )(x, mods, g_mix0, w_in_b, cos_t, sin_t, qg, kg, gv, bsum, ws2, bs_t)


def _ctx_kernel(c_ref, mod_ref, gmix_ref, wkv_ref, kg_ref, bsum_ref, kt_ref, v_ref):
    bb, n, d = c_ref.shape
    m = mod_ref[0]
    h = _norm_modulate(c_ref[...].reshape(bb * n, d), gmix_ref[...], m[0:1], m[1:2])
    proj = jnp.dot(h.astype(BF16), wkv_ref[...], preferred_element_type=F32)
    k = proj[:, :KV_WIDTH]
    k = (k * lax.rsqrt(_seg_mean_sq(k, k, bsum_ref)[0] + EPS)) * kg_ref[...]
    for j in range(bb):
        kt_ref[j] = k[j * n:(j + 1) * n].T.astype(BF16)
        _store_values_with_ones(proj[j * n:(j + 1) * n, KV_WIDTH:], v_ref, j)


def _ctx_kv(ctx, mods, g_mix0, w_kv_b, kg, bsum, ctx_row, bb):
    b, n, d = ctx.shape
    assert b % bb == 0
    return pl.pallas_call(
        _ctx_kernel,
        grid=(b // bb,),
        in_specs=[
            pl.BlockSpec((bb, n, d), lambda bi: (bi, 0, 0)),
            _mod_spec(d, 0, lambda bi: ctx_row),
            _layer_spec((1, d), 0),
            _const_spec((d, 2 * KV_WIDTH)),
            _const_spec((1, LANES)),
            _const_spec((2 * LANES, 2 * LANES)),
        ],
        out_specs=[
            pl.BlockSpec((bb, KV_WIDTH, n), lambda bi: (bi, 0, 0)),
            pl.BlockSpec((bb, n, 2 * LANES), lambda bi: (bi, 0, 0)),
        ],
        out_shape=[
            jax.ShapeDtypeStruct((b, KV_WIDTH, n), BF16),
            jax.ShapeDtypeStruct((b, n, 2 * LANES), BF16),
        ],
        compiler_params=_params(1),
        name="ctx_kv",
    )(ctx, mods, g_mix0, w_kv_b, kg, bsum)


def _attn_kernel(q_ref, kt_ref, ve_ref, kct_ref, vce_ref, qg_ref, kg_ref, w1f_ref, w3f_ref, w2f_ref,
                 o_ref, w1b_ref, w3b_ref, w2b_ref, qz_ref, r_ref):
    tq = q_ref.shape[1]
    n_heads = GQA_GROUP * N_KV_HEADS
    left = lax.broadcasted_iota(jnp.int32, (tq, LANES), 1) < HEAD_DIM
    m = (Q_SCALE * HEAD_DIM) * jnp.max(jnp.abs(qg_ref[...])) * jnp.max(jnp.abs(kg_ref[...]))

    def masked_queries(g):
        qc = q_ref[0, :, g * LANES:(g + 1) * LANES].astype(F32)
        return jnp.where(left, qc, 0.0).astype(BF16), jnp.where(left, 0.0, qc).astype(BF16)

    def normalised(r):
        return r[:, :LANES] / r[:, LANES:]

    def store_group(g, o_kh0, o_kh1):
        o_ref[0, :, g * LANES:(g + 1) * LANES] = jnp.where(left, o_kh0, o_kh1).astype(BF16)

    def key_tiles():
        for k_ref, v_ref in ((kt_ref, ve_ref), (kct_ref, vce_ref)):
            for j in range(k_ref.shape[2] // KEY_TILE):
                yield (k_ref.at[0, :, j * KEY_TILE:(j + 1) * KEY_TILE],
                       v_ref.at[0, j * KEY_TILE:(j + 1) * KEY_TILE, :])

    def cast_weight_slabs():
        for src, dst in ((w1f_ref, w1b_ref), (w3f_ref, w3b_ref), (w2f_ref, w2b_ref)):
            dst[...] = src[...].astype(BF16)

    def streamed():
        cast_weight_slabs()
        for g in range(GQA_GROUP):
            outs = []
            for qz in masked_queries(g):
                r = None
                for k_tile, v_tile in key_tiles():
                    s = jnp.dot(qz, k_tile[...], preferred_element_type=F32)
                    p = jnp.exp2(s - m).astype(BF16)
                    d = jnp.dot(p, v_tile[...], preferred_element_type=F32)
                    r = d if r is None else r + d
                outs.append(normalised(r))
            store_group(g, *outs)

    def exact_max():
        cast_weight_slabs()
        for g in range(GQA_GROUP):
            qz_ref[N_KV_HEADS * g], qz_ref[N_KV_HEADS * g + 1] = masked_queries(g)

        def head(u, carry):
            s1 = jnp.dot(qz_ref[u], kt_ref[0], preferred_element_type=F32)
            s2 = jnp.dot(qz_ref[u], kct_ref[0], preferred_element_type=F32)
            mx = jnp.maximum(jnp.max(s1, axis=-1, keepdims=True), jnp.max(s2, axis=-1, keepdims=True))
            r_ref[u] = normalised(
                jnp.dot(jnp.exp2(s1 - mx).astype(BF16), ve_ref[0], preferred_element_type=F32)
                + jnp.dot(jnp.exp2(s2 - mx).astype(BF16), vce_ref[0], preferred_element_type=F32))
            return carry
        lax.fori_loop(0, n_heads, head, 0)
        for g in range(GQA_GROUP):
            store_group(g, r_ref[N_KV_HEADS * g], r_ref[N_KV_HEADS * g + 1])

    lax.cond(m <= MAX_SAFE_SHIFT, streamed, exact_max)


def _attention(q, kt, ve, kct, vce, qg, kg, ffn_weights, tq):
    b, s, _ = q.shape
    nc = kct.shape[2]
    n_heads = GQA_GROUP * N_KV_HEADS
    assert s % KEY_TILE == 0 and nc % KEY_TILE == 0
    steps = b * (s // tq)
    per_b = s // tq
    flat = [w.reshape(-1, w.shape[-1]) for w in ffn_weights]
    slab_specs = []
    for w in flat:
        every = next(e for e in (1, 2, 4, 8) if steps % e == 0 and w.shape[0] % (steps // e * 16) == 0)
        slab_specs.append(pl.BlockSpec((w.shape[0] // (steps // every), w.shape[1]),
                                       lambda bi, i, every=every: ((bi * per_b + i) // every, 0)))
    out = pl.pallas_call(
        _attn_kernel,
        grid=(b, s // tq),
        in_specs=[
            pl.BlockSpec((1, tq, ATTN_WIDTH), lambda bi, i: (bi, i, 0)),
            pl.BlockSpec((1, KV_WIDTH, s), lambda bi, i: (bi, 0, 0)),
            pl.BlockSpec((1, s, 2 * LANES), lambda bi, i: (bi, 0, 0)),
            pl.BlockSpec((1, KV_WIDTH, nc), lambda bi, i: (bi, 0, 0)),
            pl.BlockSpec((1, nc, 2 * LANES), lambda bi, i: (bi, 0, 0)),
            _const_spec((1, LANES)),
            _const_spec((1, LANES)),
        ] + slab_specs,
        out_specs=[pl.BlockSpec((1, tq, ATTN_WIDTH), lambda bi, i: (bi, i, 0))] + slab_specs,
        out_shape=[jax.ShapeDtypeStruct((b, s, ATTN_WIDTH), BF16)]
        + [jax.ShapeDtypeStruct(w.shape, BF16) for w in flat],
        scratch_shapes=[
            pltpu.VMEM((n_heads, tq, LANES), BF16),
            pltpu.VMEM((n_heads, tq, LANES), F32),
        ],
        compiler_params=_params(2),
        name="attention",
    )(q, kt, ve, kct, vce, qg, kg, *flat)
    return out[0], [wb.reshape(w.shape) for wb, w in zip(out[1:], ffn_weights)]


def _post0_kernel(x_ref, a_ref, gm_ref, mod_ref, gffn_ref, woa_ref, wog_ref, w1_ref, w3_ref, w2_ref,
                  o_ref):
    m = mod_ref[0]
    mix = (jnp.dot(a_ref[0], woa_ref[...], preferred_element_type=F32)
           + jnp.dot(gm_ref[0], wog_ref[...], preferred_element_type=F32))
    x1 = x_ref[0] + m[2:3] * mix
    h = _norm_modulate(x1, gffn_ref[...], m[3:4], m[4:5])
    o_ref[0] = x1 + m[5:6] * _swiglu(h.astype(BF16), w1_ref, w3_ref, w2_ref)


def _post0(x, attn, gm, mods, g_ffn0, woa, wog, w1, w3, w2, layer, tm):
    b, s, d = x.shape
    dff = w1.shape[2]
    return pl.pallas_call(
        _post0_kernel,
        grid=(b, s // tm),
        in_specs=[
            pl.BlockSpec((1, tm, d), lambda bi, i: (bi, i, 0)),
            pl.BlockSpec((1, tm, ATTN_WIDTH), lambda bi, i: (bi, i, 0)),
            pl.BlockSpec((1, tm, GMLP_WIDTH), lambda bi, i: (bi, i, 0)),
            _mod_spec(d, layer, lambda bi, i: bi),
            _layer_spec((1, d), layer),
            _const_spec((ATTN_WIDTH, d)),
            _const_spec((GMLP_WIDTH, d)),
            _layer_spec((d, dff), layer),
            _layer_spec((d, dff), layer),
            _layer_spec((dff, d), layer),
        ],
        out_specs=pl.BlockSpec((1, tm, d), lambda bi, i: (bi, i, 0)),
        out_shape=jax.ShapeDtypeStruct((b, s, d), F32),
        compiler_params=_params(2),
        name="post0",
    )(x, attn, gm, mods, g_ffn0, woa, wog, w1, w3, w2)


def _layer1_kernel(x_ref, prev_ref, next_ref, mod_ref, gmix_ref, gffn_ref, ps_ref, band_ref, wp_ref,
                   w1_ref, w3_ref, w2_ref, gfin_ref, o_ref, *, seq_len):
    tm = x_ref.shape[1]
    i = pl.program_id(1)
    m = mod_ref[0]

    def norm_mod(t):
        return _norm_modulate(t, gmix_ref[...], m[0:1], m[1:2])

    x = x_ref[0]
    xn = norm_mod(x)
    xp = jnp.where(i > 0, norm_mod(prev_ref[0]), 0.0)
    xq = jnp.where(i < pl.num_programs(1) - 1, norm_mod(next_ref[0]), 0.0)
    ext = jnp.concatenate([xp, xn, xq], axis=0)
    ext_hi = ext.astype(BF16)
    ext_lo = (ext - ext_hi.astype(F32)).astype(BF16)

    sub = band_ref.shape[1]
    pos = i * tm + lax.broadcasted_iota(jnp.int32, (tm, LANES), 0)
    ys = []
    for gi, w in enumerate(POOL_WINDOWS):
        left_w = w // 2
        right_w = w - 1 - left_w
        cnt = (jnp.minimum(pos + right_w + 1, seq_len) - jnp.maximum(pos - left_w, 0)).astype(F32)
        inv_cnt = jnp.concatenate([1.0 / cnt] * (POOL_GROUP_DIM // LANES), axis=1)
        sl = slice(gi * POOL_GROUP_DIM, (gi + 1) * POOL_GROUP_DIM)
        sums = []
        for r in range(tm // sub):
            rows = slice(r * sub, r * sub + sub + 2 * HALO)
            hi_lo = jnp.concatenate([ext_hi[rows, sl], ext_lo[rows, sl]], axis=0)
            sums.append(jnp.dot(band_ref[gi], hi_lo, preferred_element_type=F32))
        pooled = jnp.concatenate(sums, axis=0) * inv_cnt - xn[:, sl]
        ys.append(jnp.dot(pooled.astype(BF16), wp_ref[gi], preferred_element_type=F32))
    y = jnp.concatenate(ys, axis=1) * ps_ref[...]
    x1 = x + m[2:3] * y
    h = _norm_modulate(x1, gffn_ref[...], m[3:4], m[4:5])
    x2 = x1 + m[5:6] * _swiglu(h.astype(BF16), w1_ref, w3_ref, w2_ref)
    o_ref[0] = _rms_rows(x2, gfin_ref[...])


def _layer1(x, mods, g_mix1, g_ffn1, pool_scale, band, wp, w1, w3, w2, g_final, layer, tm):
    b, s, d = x.shape
    dff = w1.shape[2]
    per = tm // HALO
    last = s // HALO - 1
    return pl.pallas_call(
        functools.partial(_layer1_kernel, seq_len=s),
        grid=(b, s // tm),
        in_specs=[
            pl.BlockSpec((1, tm, d), lambda bi, i: (bi, i, 0)),
            pl.BlockSpec((1, HALO, d), lambda bi, i: (bi, jnp.maximum(i * per - 1, 0), 0)),
            pl.BlockSpec((1, HALO, d), lambda bi, i: (bi, jnp.minimum((i + 1) * per, last), 0)),
            _mod_spec(d, layer, lambda bi, i: bi),
            _layer_spec((1, d), layer),
            _layer_spec((1, d), layer),
            _const_spec((1, d)),
            _const_spec(band.shape),
            _const_spec(wp.shape),
            _layer_spec((d, dff), layer),
            _layer_spec((d, dff), layer),
            _layer_spec((dff, d), layer),
            _const_spec((1, d)),
        ],
        out_specs=pl.BlockSpec((1, tm, d), lambda bi, i: (bi, i, 0)),
        out_shape=jax.ShapeDtypeStruct((b, s, d), F32),
        compiler_params=_params(2),
        name="layer1",
    )(x, x, x, mods, g_mix1, g_ffn1, pool_scale, band, wp, w1, w3, w2, g_final)


def _rope_tables(n):
    rows = n // GRID_W
    row = np.repeat(np.arange(rows), GRID_W).astype(np.float64)
    col = np.tile(np.arange(GRID_W), rows).astype(np.float64)
    half = HEAD_DIM // 2
    freqs = ROPE_THETA ** (-np.arange(0, half, 2, dtype=np.float64) / half)
    ang = np.concatenate([row[:, None] * freqs, col[:, None] * freqs], axis=-1)
    cos = np.tile(np.repeat(np.cos(ang), 2, axis=1), (1, LANES // HEAD_DIM))
    sin = np.tile(np.repeat(np.sin(ang), 2, axis=1), (1, LANES // HEAD_DIM))
    sign = np.where(np.arange(LANES) % 2 == 0, -1.0, 1.0)
    return jnp.asarray(cos, dtype=F32), jnp.asarray(sin * sign, dtype=F32)


def _band_matrices(sub):
    t = np.arange(sub)[:, None]
    e = np.arange(sub + 2 * HALO)[None, :]
    mats = []
    for w in POOL_WINDOWS:
        left = w // 2
        right = w - 1 - left
        member = ((e >= t + HALO - left) & (e <= t + HALO + right)).astype(np.float32)
        mats.append(np.concatenate([member, member], axis=1))
    return jnp.asarray(np.stack(mats), dtype=BF16)


def kernel(x, c, ctx, c_ctx, w_ada, b_ada, g_mix, g_ffn, w_in, w_out, q_norm, k_norm, gmlp_norm,
           w_spatial, b_spatial, w_pool, pool_scale, w1, w3, w2, g_final):
    b, s, d = x.shape
    depth = w_ada.shape[0]
    assert depth == 2 and d == D_MODEL and s % CHUNK == 0
    tm_pre, tq, tm_ffn = 1024, 512, 512

    cond = jnp.concatenate([c, c_ctx[None], jnp.zeros((COND_ROWS - b - 1, d), F32)], axis=0)
    mods = _adaln(cond, w_ada, b_ada).reshape(depth, COND_ROWS, 6, d)

    wi = w_in[0]
    wq = wi[:, :ATTN_WIDTH].reshape(d, N_KV_HEADS, GQA_GROUP, HEAD_DIM).transpose(0, 2, 1, 3)
    w_in_b = jnp.concatenate([wq.reshape(d, ATTN_WIDTH), wi[:, ATTN_WIDTH:]], axis=1).astype(BF16)
    w_kv_b = wi[:, ATTN_WIDTH:ATTN_WIDTH + 2 * KV_WIDTH].astype(BF16)
    wo = w_out[0]
    woa = wo[:ATTN_WIDTH].reshape(N_KV_HEADS, GQA_GROUP, HEAD_DIM, d).transpose(1, 0, 2, 3)
    woa = woa.reshape(ATTN_WIDTH, d).astype(BF16)
    wog = wo[ATTN_WIDTH:].astype(BF16)
    cos_t, sin_t = _rope_tables(s)
    qg = jnp.tile(q_norm[0], LANES // HEAD_DIM)[None]
    kg = jnp.tile(k_norm[0], LANES // HEAD_DIM)[None]
    gv = gmlp_norm[0].reshape(1, GMLP_WIDTH)
    seg = np.arange(2 * LANES) // HEAD_DIM
    bsum = jnp.asarray((seg[:, None] == seg[None, :]).astype(np.float32), dtype=BF16)
    ws = w_spatial[0]
    ws2 = jnp.concatenate([ws[0::2], ws[1::2]], axis=2).astype(BF16)
    bs_t = jnp.repeat(b_spatial[0].T, GMLP_GROUP_DIM, axis=1)

    g_mix3, g_ffn3 = g_mix.reshape(depth, 1, d), g_ffn.reshape(depth, 1, d)

    q, kt, v, gm = _pre0(x, mods, g_mix3, w_in_b, cos_t, sin_t, qg, kg, gv, bsum, ws2, bs_t, tm_pre)
    kct, vc = _ctx_kv(ctx, mods, g_mix3, w_kv_b, kg, bsum, b, CTX_BATCHES_PER_STEP)
    attn, (w1b, w3b, w2b) = _attention(q, kt, v, kct, vc, qg, kg, (w1, w3, w2), tq)
    x1 = _post0(x, attn, gm, mods, g_ffn3, woa, wog, w1b, w3b, w2b, 0, tm_ffn)

    band = _band_matrices(POOL_SUB)
    return _layer1(x1, mods, g_mix3, g_ffn3, pool_scale[:1], band,
                   w_pool[0].astype(BF16), w1b, w3b, w2b, g_final[None], 1, tm_ffn)
```

```python
import functools

import numpy as np
import jax
import jax.numpy as jnp
from jax import lax
from jax.experimental import pallas as pl
from jax.experimental.pallas import tpu as pltpu

D_MODEL = 1024
GRID_W = 64
N_HEADS = 8
N_KV_HEADS = 2
HEAD_DIM = 64
GQA_GROUP = N_HEADS // N_KV_HEADS
ATTN_WIDTH = N_HEADS * HEAD_DIM
KV_WIDTH = N_KV_HEADS * HEAD_DIM
ROPE_THETA = 10000.0
GMLP_GROUPS = 8
GMLP_GROUP_DIM = 64
GMLP_WIDTH = GMLP_GROUPS * GMLP_GROUP_DIM
CHUNK = 128
POOL_WINDOWS = (2, 4, 8, 16)
POOL_GROUP_DIM = D_MODEL // len(POOL_WINDOWS)
EPS = 1e-6
Q_SCALE = float(HEAD_DIM ** -0.5 * np.log2(np.e))

LANES = 128
HALO = 8
COND_ROWS = 16
VMEM_LIMIT = 56 * 1024 * 1024
POOL_SUB = 64
CTX_BATCHES_PER_STEP = 4
KEY_TILE = 256
MAX_SAFE_SHIFT = 60.0

F32 = jnp.float32
BF16 = jnp.bfloat16


def _const_spec(shape):
    nd = len(shape)
    return pl.BlockSpec(shape, lambda *_: (0,) * nd, pipeline_mode=pl.Buffered(1))


def _layer_spec(shape, layer):
    nd = len(shape)
    return pl.BlockSpec((None,) + tuple(shape), lambda *_: (layer,) + (0,) * nd,
                        pipeline_mode=pl.Buffered(1))


def _mod_spec(d, layer, row):
    return pl.BlockSpec((None, 1, 6, d), lambda *idx: (layer, row(*idx), 0, 0))


def _params(n_axes):
    return pltpu.CompilerParams(dimension_semantics=("arbitrary",) * n_axes,
                                vmem_limit_bytes=VMEM_LIMIT)


def _rms_rows(x, gain):
    ms = jnp.mean(x * x, axis=-1, keepdims=True)
    return (x * lax.rsqrt(ms + EPS)) * gain


def _norm_modulate(x, gain, shift, scale):
    return _rms_rows(x, gain * (1.0 + scale)) + shift


def _seg_mean_sq(ta, tb, bsum_ref):
    sq = jnp.concatenate([(ta * ta).astype(BF16), (tb * tb).astype(BF16)], axis=1)
    ss = jnp.dot(sq, bsum_ref[...], preferred_element_type=F32) * (1.0 / HEAD_DIM)
    return ss[:, :LANES], ss[:, LANES:]


def _swiglu(h, w1_ref, w3_ref, w2_ref):
    a = jnp.dot(h, w1_ref[...], preferred_element_type=F32)
    b = jnp.dot(h, w3_ref[...], preferred_element_type=F32)
    g = (a * jax.nn.sigmoid(a) * b).astype(BF16)
    return jnp.dot(g, w2_ref[...], preferred_element_type=F32)


def _adaln_kernel(cond_ref, w_ref, b_ref, o_ref):
    s = cond_ref[...]
    s = (s * jax.nn.sigmoid(s)).astype(BF16)
    o_ref[0] = jnp.dot(s, w_ref[0].astype(BF16), preferred_element_type=F32) + b_ref[0]


def _adaln(cond, w_ada, b_ada, tn=1536):
    depth, d, n = w_ada.shape
    return pl.pallas_call(
        _adaln_kernel,
        grid=(depth, n // tn),
        in_specs=[
            pl.BlockSpec((COND_ROWS, d), lambda l, j: (0, 0)),
            pl.BlockSpec((1, d, tn), lambda l, j: (l, 0, j)),
            pl.BlockSpec((1, 1, tn), lambda l, j: (l, 0, j)),
        ],
        out_specs=pl.BlockSpec((1, COND_ROWS, tn), lambda l, j: (l, 0, j)),
        out_shape=jax.ShapeDtypeStruct((depth, COND_ROWS, n), F32),
        compiler_params=_params(2),
        name="adaln",
    )(cond, w_ada, b_ada.reshape(depth, 1, n))


def _rope(t, cos, sin_signed, even_lane):
    partner = jnp.where(even_lane, pltpu.roll(t, LANES - 1, 1), pltpu.roll(t, 1, 1))
    return t * cos + partner * sin_signed


def _gelu(x):
    return 0.5 * x * (1.0 + lax.erf(x * np.float32(np.sqrt(0.5))))


def _store_values_with_ones(v, ve_ref, j=0):
    ve_ref[j, :, :LANES] = v.astype(BF16)
    ve_ref[j, :, LANES:] = jnp.ones(v.shape, BF16)


def _pre0_kernel(x_ref, mod_ref, gmix_ref, win_ref, cos_ref, sin_ref, qg_ref, kg_ref, gv_ref,
                 bsum_ref, ws_ref, bs_ref, q_ref, kt_ref, v_ref, gm_ref):
    tm = x_ref.shape[1]
    m = mod_ref[0]
    h = _norm_modulate(x_ref[0], gmix_ref[...], m[0:1], m[1:2])
    proj = jnp.dot(h.astype(BF16), win_ref[...], preferred_element_type=F32)

    lane = lax.broadcasted_iota(jnp.int32, (tm, LANES), 1)
    even_lane = (lane % 2) == 0
    cos = cos_ref[...]
    sin_signed = sin_ref[...]

    def head_norm_rope(t, mean_sq, gain):
        return _rope((t * lax.rsqrt(mean_sq + EPS)) * gain, cos, sin_signed, even_lane)

    u0 = ATTN_WIDTH + 2 * KV_WIDTH
    g0 = u0 + GMLP_WIDTH
    n_gm = GMLP_WIDTH // LANES
    blocks = [proj[:, g * LANES:(g + 1) * LANES] for g in range(GQA_GROUP)]
    blocks.append(proj[:, ATTN_WIDTH:ATTN_WIDTH + KV_WIDTH])
    blocks += [_gelu(proj[:, g0 + j * LANES:g0 + (j + 1) * LANES]) for j in range(n_gm)]
    mean_sq = []
    for a in range(0, len(blocks), 2):
        pair = _seg_mean_sq(blocks[a], blocks[min(a + 1, len(blocks) - 1)], bsum_ref)
        mean_sq += list(pair)

    for g in range(GQA_GROUP):
        q_ref[0, :, g * LANES:(g + 1) * LANES] = (
            head_norm_rope(blocks[g], mean_sq[g], qg_ref[...]) * Q_SCALE).astype(BF16)
    k = head_norm_rope(blocks[GQA_GROUP], mean_sq[GQA_GROUP], kg_ref[...])
    kt_ref[0] = k.T.astype(BF16)
    _store_values_with_ones(proj[:, ATTN_WIDTH + KV_WIDTH:ATTN_WIDTH + 2 * KV_WIDTH], v_ref)

    left = lax.broadcasted_iota(jnp.int32, (CHUNK, LANES), 1) < GMLP_GROUP_DIM
    for j in range(n_gm):
        u = _gelu(proj[:, u0 + j * LANES:u0 + (j + 1) * LANES])
        vv = blocks[GQA_GROUP + 1 + j]
        vg = (vv * lax.rsqrt(mean_sq[GQA_GROUP + 1 + j] + EPS)) * gv_ref[:, j * LANES:(j + 1) * LANES]
        bias = bs_ref[:, j * LANES:(j + 1) * LANES]
        for n in range(tm // CHUNK):
            blk = vg[n * CHUNK:(n + 1) * CHUNK]
            rhs = jnp.concatenate([jnp.where(left, blk, 0.0), jnp.where(left, 0.0, blk)],
                                  axis=0).astype(BF16)
            mixed = jnp.dot(ws_ref[j], rhs, preferred_element_type=F32) + bias
            gm_ref[0, n * CHUNK:(n + 1) * CHUNK, j * LANES:(j + 1) * LANES] = (
                u[n * CHUNK:(n + 1) * CHUNK] * mixed).astype(BF16)


def _pre0(x, mods, g_mix0, w_in_b, cos_t, sin_t, qg, kg, gv, bsum, ws2, bs_t, tm):
    b, s, d = x.shape
    nw = w_in_b.shape[1]
    return pl.pallas_call(
        _pre0_kernel,
        grid=(b, s // tm),
        in_specs=[
            pl.BlockSpec((1, tm, d), lambda bi, i: (bi, i, 0)),
            _mod_spec(d, 0, lambda bi, i: bi),
            _layer_spec((1, d), 0),
            _const_spec((d, nw)),
            pl.BlockSpec((tm, LANES), lambda bi, i: (i, 0)),
            pl.BlockSpec((tm, LANES), lambda bi, i: (i, 0)),
            _const_spec((1, LANES)),
            _const_spec((1, LANES)),
            _const_spec((1, GMLP_WIDTH)),
            _const_spec((2 * LANES, 2 * LANES)),
            _const_spec(ws2.shape),
            _const_spec(bs_t.shape),
        ],
        out_specs=[
            pl.BlockSpec((1, tm, ATTN_WIDTH), lambda bi, i: (bi, i, 0)),
            pl.BlockSpec((1, KV_WIDTH, tm), lambda bi, i: (bi, 0, i)),
            pl.BlockSpec((1, tm, 2 * LANES), lambda bi, i: (bi, i, 0)),
            pl.BlockSpec((1, tm, GMLP_WIDTH), lambda bi, i: (bi, i, 0)),
        ],
        out_shape=[
            jax.ShapeDtypeStruct((b, s, ATTN_WIDTH), BF16),
            jax.ShapeDtypeStruct((b, KV_WIDTH, s), BF16),
            jax.ShapeDtypeStruct((b, s, 2 * LANES), BF16),
            jax.ShapeDtypeStruct((b, s, GMLP_WIDTH), BF16),
        ],
        compiler_params=_params(2),
        name="pre0",
    )(x, mods, g_mix0, w_in_b, cos_t, sin_t, qg, kg, gv, bsum, ws2, bs_t)


def _ctx_kernel(c_ref, mod_ref, gmix_ref, wkv_ref, kg_ref, bsum_ref, kt_ref, v_ref):
    bb, n, d = c_ref.shape
    m = mod_ref[0]
    h = _norm_modulate(c_ref[...].reshape(bb * n, d), gmix_ref[...], m[0:1], m[1:2])
    proj = jnp.dot(h.astype(BF16), wkv_ref[...], preferred_element_type=F32)
    k = proj[:, :KV_WIDTH]
    k = (k * lax.rsqrt(_seg_mean_sq(k, k, bsum_ref)[0] + EPS)) * kg_ref[...]
    for j in range(bb):
        kt_ref[j] = k[j * n:(j + 1) * n].T.astype(BF16)
        _store_values_with_ones(proj[j * n:(j + 1) * n, KV_WIDTH:], v_ref, j)


def _ctx_kv(ctx, mods, g_mix0, w_kv_b, kg, bsum, ctx_row, bb):
    b, n, d = ctx.shape
    assert b % bb == 0
    return pl.pallas_call(
        _ctx_kernel,
        grid=(b // bb,),
        in_specs=[
            pl.BlockSpec((bb, n, d), lambda bi: (bi, 0, 0)),
            _mod_spec(d, 0, lambda bi: ctx_row),
            _layer_spec((1, d), 0),
            _const_spec((d, 2 * KV_WIDTH)),
            _const_spec((1, LANES)),
            _const_spec((2 * LANES, 2 * LANES)),
        ],
        out_specs=[
            pl.BlockSpec((bb, KV_WIDTH, n), lambda bi: (bi, 0, 0)),
            pl.BlockSpec((bb, n, 2 * LANES), lambda bi: (bi, 0, 0)),
        ],
        out_shape=[
            jax.ShapeDtypeStruct((b, KV_WIDTH, n), BF16),
            jax.ShapeDtypeStruct((b, n, 2 * LANES), BF16),
        ],
        compiler_params=_params(1),
        name="ctx_kv",
    )(ctx, mods, g_mix0, w_kv_b, kg, bsum)


def _attn_kernel(q_ref, kt_ref, ve_ref, kct_ref, vce_ref, qg_ref, kg_ref, w1f_ref, w3f_ref, w2f_ref,
                 o_ref, w1b_ref, w3b_ref, w2b_ref, qz_ref, r_ref):
    tq = q_ref.shape[1]
    n_heads = GQA_GROUP * N_KV_HEADS
    left = lax.broadcasted_iota(jnp.int32, (tq, LANES), 1) < HEAD_DIM
    m = (Q_SCALE * HEAD_DIM) * jnp.max(jnp.abs(qg_ref[...])) * jnp.max(jnp.abs(kg_ref[...]))

    def masked_queries(g):
        qc = q_ref[0, :, g * LANES:(g + 1) * LANES].astype(F32)
        return jnp.where(left, qc, 0.0).astype(BF16), jnp.where(left, 0.0, qc).astype(BF16)

    def normalised(r):
        return r[:, :LANES] / r[:, LANES:]

    def store_group(g, o_kh0, o_kh1):
        o_ref[0, :, g * LANES:(g + 1) * LANES] = jnp.where(left, o_kh0, o_kh1).astype(BF16)

    def key_tiles():
        for k_ref, v_ref in ((kt_ref, ve_ref), (kct_ref, vce_ref)):
            for j in range(k_ref.shape[2] // KEY_TILE):
                yield (k_ref.at[0, :, j * KEY_TILE:(j + 1) * KEY_TILE],
                       v_ref.at[0, j * KEY_TILE:(j + 1) * KEY_TILE, :])

    def cast_weight_slabs():
        for src, dst in ((w1f_ref, w1b_ref), (w3f_ref, w3b_ref), (w2f_ref, w2b_ref)):
            dst[...] = src[...].astype(BF16)

    def streamed():
        cast_weight_slabs()
        for g in range(GQA_GROUP):
            outs = []
            for qz in masked_queries(g):
                r = None
                for k_tile, v_tile in key_tiles():
                    s = jnp.dot(qz, k_tile[...], preferred_element_type=F32)
                    p = jnp.exp2(s - m).astype(BF16)
                    d = jnp.dot(p, v_tile[...], preferred_element_type=F32)
                    r = d if r is None else r + d
                outs.append(normalised(r))
            store_group(g, *outs)

    def exact_max():
        cast_weight_slabs()
        for g in range(GQA_GROUP):
            qz_ref[N_KV_HEADS * g], qz_ref[N_KV_HEADS * g + 1] = masked_queries(g)

        def head(u, carry):
            s1 = jnp.dot(qz_ref[u], kt_ref[0], preferred_element_type=F32)
            s2 = jnp.dot(qz_ref[u], kct_ref[0], preferred_element_type=F32)
            mx = jnp.maximum(jnp.max(s1, axis=-1, keepdims=True), jnp.max(s2, axis=-1, keepdims=True))
            r_ref[u] = normalised(
                jnp.dot(jnp.exp2(s1 - mx).astype(BF16), ve_ref[0], preferred_element_type=F32)
                + jnp.dot(jnp.exp2(s2 - mx).astype(BF16), vce_ref[0], preferred_element_type=F32))
            return carry
        lax.fori_loop(0, n_heads, head, 0)
        for g in range(GQA_GROUP):
            store_group(g, r_ref[N_KV_HEADS * g], r_ref[N_KV_HEADS * g + 1])

    lax.cond(m <= MAX_SAFE_SHIFT, streamed, exact_max)


def _attention(q, kt, ve, kct, vce, qg, kg, ffn_weights, tq):
    b, s, _ = q.shape
    nc = kct.shape[2]
    n_heads = GQA_GROUP * N_KV_HEADS
    assert s % KEY_TILE == 0 and nc % KEY_TILE == 0
    steps = b * (s // tq)
    per_b = s // tq
    slab_specs = []
    for w in ffn_weights:
        layers, rows, cols = w.shape
        every = next(e for e in (1, 2, 4, 8)
                     if steps % (e * layers) == 0 and rows % (steps // (e * layers) * 16) == 0)
        per_layer = steps // (every * layers)
        slab_specs.append(pl.BlockSpec(
            (1, rows // per_layer, cols),
            lambda bi, i, every=every, per_layer=per_layer: (
                (bi * per_b + i) // every // per_layer, (bi * per_b + i) // every % per_layer, 0)))
    out = pl.pallas_call(
        _attn_kernel,
        grid=(b, s // tq),
        in_specs=[
            pl.BlockSpec((1, tq, ATTN_WIDTH), lambda bi, i: (bi, i, 0)),
            pl.BlockSpec((1, KV_WIDTH, s), lambda bi, i: (bi, 0, 0)),
            pl.BlockSpec((1, s, 2 * LANES), lambda bi, i: (bi, 0, 0)),
            pl.BlockSpec((1, KV_WIDTH, nc), lambda bi, i: (bi, 0, 0)),
            pl.BlockSpec((1, nc, 2 * LANES), lambda bi, i: (bi, 0, 0)),
            _const_spec((1, LANES)),
            _const_spec((1, LANES)),
        ] + slab_specs,
        out_specs=[pl.BlockSpec((1, tq, ATTN_WIDTH), lambda bi, i: (bi, i, 0))] + slab_specs,
        out_shape=[jax.ShapeDtypeStruct((b, s, ATTN_WIDTH), BF16)]
        + [jax.ShapeDtypeStruct(w.shape, BF16) for w in ffn_weights],
        scratch_shapes=[
            pltpu.VMEM((n_heads, tq, LANES), BF16),
            pltpu.VMEM((n_heads, tq, LANES), F32),
        ],
        compiler_params=_params(2),
        name="attention",
    )(q, kt, ve, kct, vce, qg, kg, *ffn_weights)
    return out[0], out[1:]


def _post0_kernel(x_ref, a_ref, gm_ref, mod_ref, gffn_ref, woa_ref, wog_ref, w1_ref, w3_ref, w2_ref,
                  o_ref):
    m = mod_ref[0]
    mix = (jnp.dot(a_ref[0], woa_ref[...], preferred_element_type=F32)
           + jnp.dot(gm_ref[0], wog_ref[...], preferred_element_type=F32))
    x1 = x_ref[0] + m[2:3] * mix
    h = _norm_modulate(x1, gffn_ref[...], m[3:4], m[4:5])
    o_ref[0] = x1 + m[5:6] * _swiglu(h.astype(BF16), w1_ref, w3_ref, w2_ref)


def _post0(x, attn, gm, mods, g_ffn0, woa, wog, w1, w3, w2, layer, tm):
    b, s, d = x.shape
    dff = w1.shape[2]
    return pl.pallas_call(
        _post0_kernel,
        grid=(b, s // tm),
        in_specs=[
            pl.BlockSpec((1, tm, d), lambda bi, i: (bi, i, 0)),
            pl.BlockSpec((1, tm, ATTN_WIDTH), lambda bi, i: (bi, i, 0)),
            pl.BlockSpec((1, tm, GMLP_WIDTH), lambda bi, i: (bi, i, 0)),
            _mod_spec(d, layer, lambda bi, i: bi),
            _layer_spec((1, d), layer),
            _const_spec((ATTN_WIDTH, d)),
            _const_spec((GMLP_WIDTH, d)),
            _layer_spec((d, dff), layer),
            _layer_spec((d, dff), layer),
            _layer_spec((dff, d), layer),
        ],
        out_specs=pl.BlockSpec((1, tm, d), lambda bi, i: (bi, i, 0)),
        out_shape=jax.ShapeDtypeStruct((b, s, d), F32),
        compiler_params=_params(2),
        name="post0",
    )(x, attn, gm, mods, g_ffn0, woa, wog, w1, w3, w2)


def _layer1_kernel(x_ref, prev_ref, next_ref, mod_ref, gmix_ref, gffn_ref, ps_ref, band_ref, wp_ref,
                   w1_ref, w3_ref, w2_ref, gfin_ref, o_ref, *, seq_len):
    tm = x_ref.shape[1]
    i = pl.program_id(1)
    m = mod_ref[0]

    def norm_mod(t):
        return _norm_modulate(t, gmix_ref[...], m[0:1], m[1:2])

    x = x_ref[0]
    xn = norm_mod(x)
    xp = jnp.where(i > 0, norm_mod(prev_ref[0]), 0.0)
    xq = jnp.where(i < pl.num_programs(1) - 1, norm_mod(next_ref[0]), 0.0)
    ext = jnp.concatenate([xp, xn, xq], axis=0)
    ext_hi = ext.astype(BF16)
    ext_lo = (ext - ext_hi.astype(F32)).astype(BF16)

    sub = band_ref.shape[1]
    pos = i * tm + lax.broadcasted_iota(jnp.int32, (tm, LANES), 0)
    ys = []
    for gi, w in enumerate(POOL_WINDOWS):
        left_w = w // 2
        right_w = w - 1 - left_w
        cnt = (jnp.minimum(pos + right_w + 1, seq_len) - jnp.maximum(pos - left_w, 0)).astype(F32)
        inv_cnt = jnp.concatenate([1.0 / cnt] * (POOL_GROUP_DIM // LANES), axis=1)
        sl = slice(gi * POOL_GROUP_DIM, (gi + 1) * POOL_GROUP_DIM)
        sums = []
        for r in range(tm // sub):
            rows = slice(r * sub, r * sub + sub + 2 * HALO)
            hi_lo = jnp.concatenate([ext_hi[rows, sl], ext_lo[rows, sl]], axis=0)
            sums.append(jnp.dot(band_ref[gi], hi_lo, preferred_element_type=F32))
        pooled = jnp.concatenate(sums, axis=0) * inv_cnt - xn[:, sl]
        ys.append(jnp.dot(pooled.astype(BF16), wp_ref[gi], preferred_element_type=F32))
    y = jnp.concatenate(ys, axis=1) * ps_ref[...]
    x1 = x + m[2:3] * y
    h = _norm_modulate(x1, gffn_ref[...], m[3:4], m[4:5])
    x2 = x1 + m[5:6] * _swiglu(h.astype(BF16), w1_ref, w3_ref, w2_ref)
    o_ref[0] = _rms_rows(x2, gfin_ref[...])


def _layer1(x, mods, g_mix1, g_ffn1, pool_scale, band, wp, w1, w3, w2, g_final, layer, tm):
    b, s, d = x.shape
    dff = w1.shape[2]
    per = tm // HALO
    last = s // HALO - 1
    return pl.pallas_call(
        functools.partial(_layer1_kernel, seq_len=s),
        grid=(b, s // tm),
        in_specs=[
            pl.BlockSpec((1, tm, d), lambda bi, i: (bi, i, 0)),
            pl.BlockSpec((1, HALO, d), lambda bi, i: (bi, jnp.maximum(i * per - 1, 0), 0)),
            pl.BlockSpec((1, HALO, d), lambda bi, i: (bi, jnp.minimum((i + 1) * per, last), 0)),
            _mod_spec(d, layer, lambda bi, i: bi),
            _layer_spec((1, d), layer),
            _layer_spec((1, d), layer),
            _const_spec((1, d)),
            _const_spec(band.shape),
            _const_spec(wp.shape),
            _layer_spec((d, dff), layer),
            _layer_spec((d, dff), layer),
            _layer_spec((dff, d), layer),
            _const_spec((1, d)),
        ],
        out_specs=pl.BlockSpec((1, tm, d), lambda bi, i: (bi, i, 0)),
        out_shape=jax.ShapeDtypeStruct((b, s, d), F32),
        compiler_params=_params(2),
        name="layer1",
    )(x, x, x, mods, g_mix1, g_ffn1, pool_scale, band, wp, w1, w3, w2, g_final)


def _rope_tables(n):
    rows = n // GRID_W
    row = np.repeat(np.arange(rows), GRID_W).astype(np.float64)
    col = np.tile(np.arange(GRID_W), rows).astype(np.float64)
    half = HEAD_DIM // 2
    freqs = ROPE_THETA ** (-np.arange(0, half, 2, dtype=np.float64) / half)
    ang = np.concatenate([row[:, None] * freqs, col[:, None] * freqs], axis=-1)
    cos = np.tile(np.repeat(np.cos(ang), 2, axis=1), (1, LANES // HEAD_DIM))
    sin = np.tile(np.repeat(np.sin(ang), 2, axis=1), (1, LANES // HEAD_DIM))
    sign = np.where(np.arange(LANES) % 2 == 0, -1.0, 1.0)
    return jnp.asarray(cos, dtype=F32), jnp.asarray(sin * sign, dtype=F32)


def _band_matrices(sub):
    t = np.arange(sub)[:, None]
    e = np.arange(sub + 2 * HALO)[None, :]
    mats = []
    for w in POOL_WINDOWS:
        left = w // 2
        right = w - 1 - left
        member = ((e >= t + HALO - left) & (e <= t + HALO + right)).astype(np.float32)
        mats.append(np.concatenate([member, member], axis=1))
    return jnp.asarray(np.stack(mats), dtype=BF16)


def kernel(x, c, ctx, c_ctx, w_ada, b_ada, g_mix, g_ffn, w_in, w_out, q_norm, k_norm, gmlp_norm,
           w_spatial, b_spatial, w_pool, pool_scale, w1, w3, w2, g_final):
    b, s, d = x.shape
    depth = w_ada.shape[0]
    assert depth == 2 and d == D_MODEL and s % CHUNK == 0
    tm_pre, tq, tm_ffn = 1024, 512, 512

    cond = jnp.concatenate([c, c_ctx[None], jnp.zeros((COND_ROWS - b - 1, d), F32)], axis=0)
    mods = _adaln(cond, w_ada, b_ada).reshape(depth, COND_ROWS, 6, d)

    wi = w_in[0]
    wq = wi[:, :ATTN_WIDTH].reshape(d, N_KV_HEADS, GQA_GROUP, HEAD_DIM).transpose(0, 2, 1, 3)
    w_in_b = jnp.concatenate([wq.reshape(d, ATTN_WIDTH), wi[:, ATTN_WIDTH:]], axis=1).astype(BF16)
    w_kv_b = wi[:, ATTN_WIDTH:ATTN_WIDTH + 2 * KV_WIDTH].astype(BF16)
    wo = w_out[0]
    woa = wo[:ATTN_WIDTH].reshape(N_KV_HEADS, GQA_GROUP, HEAD_DIM, d).transpose(1, 0, 2, 3)
    woa = woa.reshape(ATTN_WIDTH, d).astype(BF16)
    wog = wo[ATTN_WIDTH:].astype(BF16)
    cos_t, sin_t = _rope_tables(s)
    qg = jnp.tile(q_norm[0], LANES // HEAD_DIM)[None]
    kg = jnp.tile(k_norm[0], LANES // HEAD_DIM)[None]
    gv = gmlp_norm[0].reshape(1, GMLP_WIDTH)
    seg = np.arange(2 * LANES) // HEAD_DIM
    bsum = jnp.asarray((seg[:, None] == seg[None, :]).astype(np.float32), dtype=BF16)
    ws = w_spatial[0]
    ws2 = jnp.concatenate([ws[0::2], ws[1::2]], axis=2).astype(BF16)
    bs_t = jnp.repeat(b_spatial[0].T, GMLP_GROUP_DIM, axis=1)

    g_mix3, g_ffn3 = g_mix.reshape(depth, 1, d), g_ffn.reshape(depth, 1, d)

    q, kt, v, gm = _pre0(x, mods, g_mix3, w_in_b, cos_t, sin_t, qg, kg, gv, bsum, ws2, bs_t, tm_pre)
    kct, vc = _ctx_kv(ctx, mods, g_mix3, w_kv_b, kg, bsum, b, CTX_BATCHES_PER_STEP)
    attn, (w1b, w3b, w2b) = _attention(q, kt, v, kct, vc, qg, kg, (w1, w3, w2), tq)
    x1 = _post0(x, attn, gm, mods, g_ffn3, woa, wog, w1b, w3b, w2b, 0, tm_ffn)

    band = _band_matrices(POOL_SUB)
    return _layer1(x1, mods, g_mix3, g_ffn3, pool_scale[:1], band,
                   w_pool[0].astype(BF16), w1b, w3b, w2b, g_final[None], 1, tm_ffn)
```

```python
import functools

import numpy as np
import jax
import jax.numpy as jnp
from jax import lax
from jax.experimental import pallas as pl
from jax.experimental.pallas import tpu as pltpu

D_MODEL = 1024
GRID_W = 64
N_HEADS = 8
N_KV_HEADS = 2
HEAD_DIM = 64
GQA_GROUP = N_HEADS // N_KV_HEADS
ATTN_WIDTH = N_HEADS * HEAD_DIM
KV_WIDTH = N_KV_HEADS * HEAD_DIM
ROPE_THETA = 10000.0
GMLP_GROUPS = 8
GMLP_GROUP_DIM = 64
GMLP_WIDTH = GMLP_GROUPS * GMLP_GROUP_DIM
CHUNK = 128
POOL_WINDOWS = (2, 4, 8, 16)
POOL_GROUP_DIM = D_MODEL // len(POOL_WINDOWS)
EPS = 1e-6
Q_SCALE = float(HEAD_DIM ** -0.5 * np.log2(np.e))

LANES = 128
HALO = 8
COND_ROWS = 16
VMEM_LIMIT = 56 * 1024 * 1024
POOL_SUB = 64
CTX_BATCHES_PER_STEP = 4
KEY_TILE = 256
FFN_CHUNK = 768
MAX_SAFE_SHIFT = 60.0

F32 = jnp.float32
BF16 = jnp.bfloat16


def _const_spec(shape):
    nd = len(shape)
    return pl.BlockSpec(shape, lambda *_: (0,) * nd, pipeline_mode=pl.Buffered(1))


def _layer_spec(shape, layer):
    nd = len(shape)
    return pl.BlockSpec((None,) + tuple(shape), lambda *_: (layer,) + (0,) * nd,
                        pipeline_mode=pl.Buffered(1))


def _mod_spec(d, layer, row):
    return pl.BlockSpec((None, 1, 6, d), lambda *idx: (layer, row(*idx), 0, 0))


def _params(n_axes):
    return pltpu.CompilerParams(dimension_semantics=("arbitrary",) * n_axes,
                                vmem_limit_bytes=VMEM_LIMIT)


def _rms_rows(x, gain):
    ms = jnp.mean(x * x, axis=-1, keepdims=True)
    return (x * lax.rsqrt(ms + EPS)) * gain


def _norm_modulate(x, gain, shift, scale):
    return _rms_rows(x, gain * (1.0 + scale)) + shift


def _seg_mean_sq(ta, tb, bsum_ref):
    sq = jnp.concatenate([(ta * ta).astype(BF16), (tb * tb).astype(BF16)], axis=1)
    ss = jnp.dot(sq, bsum_ref[...], preferred_element_type=F32) * (1.0 / HEAD_DIM)
    return ss[:, :LANES], ss[:, LANES:]


def _swiglu(h, w1_ref, w3_ref, w2_ref):
    dff = w1_ref.shape[1]
    f = None
    for c0 in range(0, dff, FFN_CHUNK):
        c1 = min(c0 + FFN_CHUNK, dff)
        a = jnp.dot(h, w1_ref[:, c0:c1], preferred_element_type=F32)
        b = jnp.dot(h, w3_ref[:, c0:c1], preferred_element_type=F32)
        g = (a * jax.nn.sigmoid(a) * b).astype(BF16)
        d = jnp.dot(g, w2_ref[c0:c1, :], preferred_element_type=F32)
        f = d if f is None else f + d
    return f


def _adaln_kernel(cond_ref, w_ref, b_ref, o_ref):
    s = cond_ref[...]
    s = (s * jax.nn.sigmoid(s)).astype(BF16)
    o_ref[0] = jnp.dot(s, w_ref[0].astype(BF16), preferred_element_type=F32) + b_ref[0]


def _adaln(cond, w_ada, b_ada, tn=1536):
    depth, d, n = w_ada.shape
    return pl.pallas_call(
        _adaln_kernel,
        grid=(depth, n // tn),
        in_specs=[
            pl.BlockSpec((COND_ROWS, d), lambda l, j: (0, 0)),
            pl.BlockSpec((1, d, tn), lambda l, j: (l, 0, j)),
            pl.BlockSpec((1, 1, tn), lambda l, j: (l, 0, j)),
        ],
        out_specs=pl.BlockSpec((1, COND_ROWS, tn), lambda l, j: (l, 0, j)),
        out_shape=jax.ShapeDtypeStruct((depth, COND_ROWS, n), F32),
        compiler_params=_params(2),
        name="adaln",
    )(cond, w_ada, b_ada.reshape(depth, 1, n))


def _rope(t, cos, sin_signed, even_lane):
    partner = jnp.where(even_lane, pltpu.roll(t, LANES - 1, 1), pltpu.roll(t, 1, 1))
    return t * cos + partner * sin_signed


def _gelu(x):
    return 0.5 * x * (1.0 + lax.erf(x * np.float32(np.sqrt(0.5))))


def _store_values_with_ones(v, ve_ref, j=0):
    ve_ref[j, :, :LANES] = v.astype(BF16)
    ve_ref[j, :, LANES:] = jnp.ones(v.shape, BF16)


def _pre0_kernel(x_ref, mod_ref, gmix_ref, win_ref, cos_ref, sin_ref, qg_ref, kg_ref, gv_ref,
                 bsum_ref, ws_ref, bs_ref, q_ref, kt_ref, v_ref, gm_ref):
    tm = x_ref.shape[1]
    m = mod_ref[0]
    h = _norm_modulate(x_ref[0], gmix_ref[...], m[0:1], m[1:2])
    proj = jnp.dot(h.astype(BF16), win_ref[...], preferred_element_type=F32)

    lane = lax.broadcasted_iota(jnp.int32, (tm, LANES), 1)
    even_lane = (lane % 2) == 0
    cos = cos_ref[...]
    sin_signed = sin_ref[...]

    def head_norm_rope(t, mean_sq, gain):
        return _rope((t * lax.rsqrt(mean_sq + EPS)) * gain, cos, sin_signed, even_lane)

    u0 = ATTN_WIDTH + 2 * KV_WIDTH
    g0 = u0 + GMLP_WIDTH
    n_gm = GMLP_WIDTH // LANES
    blocks = [proj[:, g * LANES:(g + 1) * LANES] for g in range(GQA_GROUP)]
    blocks.append(proj[:, ATTN_WIDTH:ATTN_WIDTH + KV_WIDTH])
    blocks += [_gelu(proj[:, g0 + j * LANES:g0 + (j + 1) * LANES]) for j in range(n_gm)]
    mean_sq = []
    for a in range(0, len(blocks), 2):
        pair = _seg_mean_sq(blocks[a], blocks[min(a + 1, len(blocks) - 1)], bsum_ref)
        mean_sq += list(pair)

    for g in range(GQA_GROUP):
        q_ref[0, :, g * LANES:(g + 1) * LANES] = (
            head_norm_rope(blocks[g], mean_sq[g], qg_ref[...]) * Q_SCALE).astype(BF16)
    k = head_norm_rope(blocks[GQA_GROUP], mean_sq[GQA_GROUP], kg_ref[...])
    kt_ref[0] = k.T.astype(BF16)
    _store_values_with_ones(proj[:, ATTN_WIDTH + KV_WIDTH:ATTN_WIDTH + 2 * KV_WIDTH], v_ref)

    left = lax.broadcasted_iota(jnp.int32, (CHUNK, LANES), 1) < GMLP_GROUP_DIM
    for j in range(n_gm):
        u = _gelu(proj[:, u0 + j * LANES:u0 + (j + 1) * LANES])
        vv = blocks[GQA_GROUP + 1 + j]
        vg = (vv * lax.rsqrt(mean_sq[GQA_GROUP + 1 + j] + EPS)) * gv_ref[:, j * LANES:(j + 1) * LANES]
        bias = bs_ref[:, j * LANES:(j + 1) * LANES]
        for n in range(tm // CHUNK):
            blk = vg[n * CHUNK:(n + 1) * CHUNK]
            rhs = jnp.concatenate([jnp.where(left, blk, 0.0), jnp.where(left, 0.0, blk)],
                                  axis=0).astype(BF16)
            mixed = jnp.dot(ws_ref[j], rhs, preferred_element_type=F32) + bias
            gm_ref[0, n * CHUNK:(n + 1) * CHUNK, j * LANES:(j + 1) * LANES] = (
                u[n * CHUNK:(n + 1) * CHUNK] * mixed).astype(BF16)


def _pre0(x, mods, g_mix0, w_in_b, cos_t, sin_t, qg, kg, gv, bsum, ws2, bs_t, tm):
    b, s, d = x.shape
    nw = w_in_b.shape[1]
    return pl.pallas_call(
        _pre0_kernel,
        grid=(b, s // tm),
        in_specs=[
            pl.BlockSpec((1, tm, d), lambda bi, i: (bi, i, 0)),
            _mod_spec(d, 0, lambda bi, i: bi),
            _layer_spec((1, d), 0),
            _const_spec((d, nw)),
            pl.BlockSpec((tm, LANES), lambda bi, i: (i, 0)),
            pl.BlockSpec((tm, LANES), lambda bi, i: (i, 0)),
            _const_spec((1, LANES)),
            _const_spec((1, LANES)),
            _const_spec((1, GMLP_WIDTH)),
            _const_spec((2 * LANES, 2 * LANES)),
            _const_spec(ws2.shape),
            _const_spec(bs_t.shape),
        ],
        out_specs=[
            pl.BlockSpec((1, tm, ATTN_WIDTH), lambda bi, i: (bi, i, 0)),
            pl.BlockSpec((1, KV_WIDTH, tm), lambda bi, i: (bi, 0, i)),
            pl.BlockSpec((1, tm, 2 * LANES), lambda bi, i: (bi, i, 0)),
            pl.BlockSpec((1, tm, GMLP_WIDTH), lambda bi, i: (bi, i, 0)),
        ],
        out_shape=[
            jax.ShapeDtypeStruct((b, s, ATTN_WIDTH), BF16),
            jax.ShapeDtypeStruct((b, KV_WIDTH, s), BF16),
            jax.ShapeDtypeStruct((b, s, 2 * LANES), BF16),
            jax.ShapeDtypeStruct((b, s, GMLP_WIDTH), BF16),
        ],
        compiler_params=_params(2),
        name="pre0",
    )(x, mods, g_mix0, w_in_b, cos_t, sin_t, qg, kg, gv, bsum, ws2, bs_t)


def _ctx_kernel(c_ref, mod_ref, gmix_ref, wkv_ref, kg_ref, bsum_ref, kt_ref, v_ref):
    bb, n, d = c_ref.shape
    m = mod_ref[0]
    h = _norm_modulate(c_ref[...].reshape(bb * n, d), gmix_ref[...], m[0:1], m[1:2])
    proj = jnp.dot(h.astype(BF16), wkv_ref[...], preferred_element_type=F32)
    k = proj[:, :KV_WIDTH]
    k = (k * lax.rsqrt(_seg_mean_sq(k, k, bsum_ref)[0] + EPS)) * kg_ref[...]
    for j in range(bb):
        kt_ref[j] = k[j * n:(j + 1) * n].T.astype(BF16)
        _store_values_with_ones(proj[j * n:(j + 1) * n, KV_WIDTH:], v_ref, j)


def _ctx_kv(ctx, mods, g_mix0, w_kv_b, kg, bsum, ctx_row, bb):
    b, n, d = ctx.shape
    assert b % bb == 0
    return pl.pallas_call(
        _ctx_kernel,
        grid=(b // bb,),
        in_specs=[
            pl.BlockSpec((bb, n, d), lambda bi: (bi, 0, 0)),
            _mod_spec(d, 0, lambda bi: ctx_row),
            _layer_spec((1, d), 0),
            _const_spec((d, 2 * KV_WIDTH)),
            _const_spec((1, LANES)),
            _const_spec((2 * LANES, 2 * LANES)),
        ],
        out_specs=[
            pl.BlockSpec((bb, KV_WIDTH, n), lambda bi: (bi, 0, 0)),
            pl.BlockSpec((bb, n, 2 * LANES), lambda bi: (bi, 0, 0)),
        ],
        out_shape=[
            jax.ShapeDtypeStruct((b, KV_WIDTH, n), BF16),
            jax.ShapeDtypeStruct((b, n, 2 * LANES), BF16),
        ],
        compiler_params=_params(1),
        name="ctx_kv",
    )(ctx, mods, g_mix0, w_kv_b, kg, bsum)


def _attn_kernel(q_ref, kt_ref, ve_ref, kct_ref, vce_ref, qg_ref, kg_ref, w1f_ref, w3f_ref, w2f_ref,
                 o_ref, w1b_ref, w3b_ref, w2b_ref, qz_ref, r_ref):
    tq = q_ref.shape[1]
    n_heads = GQA_GROUP * N_KV_HEADS
    left = lax.broadcasted_iota(jnp.int32, (tq, LANES), 1) < HEAD_DIM
    m = (Q_SCALE * HEAD_DIM) * jnp.max(jnp.abs(qg_ref[...])) * jnp.max(jnp.abs(kg_ref[...]))

    def masked_queries(g):
        qc = q_ref[0, :, g * LANES:(g + 1) * LANES].astype(F32)
        return jnp.where(left, qc, 0.0).astype(BF16), jnp.where(left, 0.0, qc).astype(BF16)

    def normalised(r):
        return r[:, :LANES] / r[:, LANES:]

    def store_group(g, o_kh0, o_kh1):
        o_ref[0, :, g * LANES:(g + 1) * LANES] = jnp.where(left, o_kh0, o_kh1).astype(BF16)

    def key_tiles():
        for k_ref, v_ref in ((kt_ref, ve_ref), (kct_ref, vce_ref)):
            for j in range(k_ref.shape[2] // KEY_TILE):
                yield (k_ref.at[0, :, j * KEY_TILE:(j + 1) * KEY_TILE],
                       v_ref.at[0, j * KEY_TILE:(j + 1) * KEY_TILE, :])

    def cast_weight_slabs():
        for src, dst in ((w1f_ref, w1b_ref), (w3f_ref, w3b_ref), (w2f_ref, w2b_ref)):
            dst[...] = src[...].astype(BF16)

    def streamed():
        cast_weight_slabs()
        for g in range(GQA_GROUP):
            outs = []
            for qz in masked_queries(g):
                r = None
                for k_tile, v_tile in key_tiles():
                    s = jnp.dot(qz, k_tile[...], preferred_element_type=F32)
                    p = jnp.exp2(s - m).astype(BF16)
                    d = jnp.dot(p, v_tile[...], preferred_element_type=F32)
                    r = d if r is None else r + d
                outs.append(normalised(r))
            store_group(g, *outs)

    def exact_max():
        cast_weight_slabs()
        for g in range(GQA_GROUP):
            qz_ref[N_KV_HEADS * g], qz_ref[N_KV_HEADS * g + 1] = masked_queries(g)

        def head(u, carry):
            s1 = jnp.dot(qz_ref[u], kt_ref[0], preferred_element_type=F32)
            s2 = jnp.dot(qz_ref[u], kct_ref[0], preferred_element_type=F32)
            mx = jnp.maximum(jnp.max(s1, axis=-1, keepdims=True), jnp.max(s2, axis=-1, keepdims=True))
            r_ref[u] = normalised(
                jnp.dot(jnp.exp2(s1 - mx).astype(BF16), ve_ref[0], preferred_element_type=F32)
                + jnp.dot(jnp.exp2(s2 - mx).astype(BF16), vce_ref[0], preferred_element_type=F32))
            return carry
        lax.fori_loop(0, n_heads, head, 0)
        for g in range(GQA_GROUP):
            store_group(g, r_ref[N_KV_HEADS * g], r_ref[N_KV_HEADS * g + 1])

    lax.cond(m <= MAX_SAFE_SHIFT, streamed, exact_max)


def _attention(q, kt, ve, kct, vce, qg, kg, ffn_weights, tq):
    b, s, _ = q.shape
    nc = kct.shape[2]
    n_heads = GQA_GROUP * N_KV_HEADS
    assert s % KEY_TILE == 0 and nc % KEY_TILE == 0
    steps = b * (s // tq)
    per_b = s // tq
    flat = [w.reshape(-1, w.shape[-1]) for w in ffn_weights]
    slab_specs = []
    for w in flat:
        every = next(e for e in (1, 2, 4, 8) if steps % e == 0 and w.shape[0] % (steps // e * 16) == 0)
        slab_specs.append(pl.BlockSpec((w.shape[0] // (steps // every), w.shape[1]),
                                       lambda bi, i, every=every: ((bi * per_b + i) // every, 0)))
    out = pl.pallas_call(
        _attn_kernel,
        grid=(b, s // tq),
        in_specs=[
            pl.BlockSpec((1, tq, ATTN_WIDTH), lambda bi, i: (bi, i, 0)),
            pl.BlockSpec((1, KV_WIDTH, s), lambda bi, i: (bi, 0, 0)),
            pl.BlockSpec((1, s, 2 * LANES), lambda bi, i: (bi, 0, 0)),
            pl.BlockSpec((1, KV_WIDTH, nc), lambda bi, i: (bi, 0, 0)),
            pl.BlockSpec((1, nc, 2 * LANES), lambda bi, i: (bi, 0, 0)),
            _const_spec((1, LANES)),
            _const_spec((1, LANES)),
        ] + slab_specs,
        out_specs=[pl.BlockSpec((1, tq, ATTN_WIDTH), lambda bi, i: (bi, i, 0))] + slab_specs,
        out_shape=[jax.ShapeDtypeStruct((b, s, ATTN_WIDTH), BF16)]
        + [jax.ShapeDtypeStruct(w.shape, BF16) for w in flat],
        scratch_shapes=[
            pltpu.VMEM((n_heads, tq, LANES), BF16),
            pltpu.VMEM((n_heads, tq, LANES), F32),
        ],
        compiler_params=_params(2),
        name="attention",
    )(q, kt, ve, kct, vce, qg, kg, *flat)
    return out[0], [wb.reshape(w.shape) for wb, w in zip(out[1:], ffn_weights)]


def _post0_kernel(x_ref, a_ref, gm_ref, mod_ref, gffn_ref, woa_ref, wog_ref, w1_ref, w3_ref, w2_ref,
                  o_ref):
    m = mod_ref[0]
    mix = (jnp.dot(a_ref[0], woa_ref[...], preferred_element_type=F32)
           + jnp.dot(gm_ref[0], wog_ref[...], preferred_element_type=F32))
    x1 = x_ref[0] + m[2:3] * mix
    h = _norm_modulate(x1, gffn_ref[...], m[3:4], m[4:5])
    o_ref[0] = x1 + m[5:6] * _swiglu(h.astype(BF16), w1_ref, w3_ref, w2_ref)


def _post0(x, attn, gm, mods, g_ffn0, woa, wog, w1, w3, w2, layer, tm):
    b, s, d = x.shape
    dff = w1.shape[2]
    return pl.pallas_call(
        _post0_kernel,
        grid=(b, s // tm),
        in_specs=[
            pl.BlockSpec((1, tm, d), lambda bi, i: (bi, i, 0)),
            pl.BlockSpec((1, tm, ATTN_WIDTH), lambda bi, i: (bi, i, 0)),
            pl.BlockSpec((1, tm, GMLP_WIDTH), lambda bi, i: (bi, i, 0)),
            _mod_spec(d, layer, lambda bi, i: bi),
            _layer_spec((1, d), layer),
            _const_spec((ATTN_WIDTH, d)),
            _const_spec((GMLP_WIDTH, d)),
            _layer_spec((d, dff), layer),
            _layer_spec((d, dff), layer),
            _layer_spec((dff, d), layer),
        ],
        out_specs=pl.BlockSpec((1, tm, d), lambda bi, i: (bi, i, 0)),
        out_shape=jax.ShapeDtypeStruct((b, s, d), F32),
        compiler_params=_params(2),
        name="post0",
    )(x, attn, gm, mods, g_ffn0, woa, wog, w1, w3, w2)


def _layer1_kernel(x_ref, prev_ref, next_ref, mod_ref, gmix_ref, gffn_ref, ps_ref, band_ref, wp_ref,
                   w1_ref, w3_ref, w2_ref, gfin_ref, o_ref, *, seq_len):
    tm = x_ref.shape[1]
    i = pl.program_id(1)
    m = mod_ref[0]

    def norm_mod(t):
        return _norm_modulate(t, gmix_ref[...], m[0:1], m[1:2])

    x = x_ref[0]
    xn = norm_mod(x)
    xp = jnp.where(i > 0, norm_mod(prev_ref[0]), 0.0)
    xq = jnp.where(i < pl.num_programs(1) - 1, norm_mod(next_ref[0]), 0.0)
    ext = jnp.concatenate([xp, xn, xq], axis=0)
    ext_hi = ext.astype(BF16)
    ext_lo = (ext - ext_hi.astype(F32)).astype(BF16)

    sub = band_ref.shape[1]
    pos = i * tm + lax.broadcasted_iota(jnp.int32, (tm, LANES), 0)
    ys = []
    for gi, w in enumerate(POOL_WINDOWS):
        left_w = w // 2
        right_w = w - 1 - left_w
        cnt = (jnp.minimum(pos + right_w + 1, seq_len) - jnp.maximum(pos - left_w, 0)).astype(F32)
        inv_cnt = jnp.concatenate([1.0 / cnt] * (POOL_GROUP_DIM // LANES), axis=1)
        sl = slice(gi * POOL_GROUP_DIM, (gi + 1) * POOL_GROUP_DIM)
        sums = []
        for r in range(tm // sub):
            rows = slice(r * sub, r * sub + sub + 2 * HALO)
            hi_lo = jnp.concatenate([ext_hi[rows, sl], ext_lo[rows, sl]], axis=0)
            sums.append(jnp.dot(band_ref[gi], hi_lo, preferred_element_type=F32))
        pooled = jnp.concatenate(sums, axis=0) * inv_cnt - xn[:, sl]
        ys.append(jnp.dot(pooled.astype(BF16), wp_ref[gi], preferred_element_type=F32))
    y = jnp.concatenate(ys, axis=1) * ps_ref[...]
    x1 = x + m[2:3] * y
    h = _norm_modulate(x1, gffn_ref[...], m[3:4], m[4:5])
    x2 = x1 + m[5:6] * _swiglu(h.astype(BF16), w1_ref, w3_ref, w2_ref)
    o_ref[0] = _rms_rows(x2, gfin_ref[...])


def _layer1(x, mods, g_mix1, g_ffn1, pool_scale, band, wp, w1, w3, w2, g_final, layer, tm):
    b, s, d = x.shape
    dff = w1.shape[2]
    per = tm // HALO
    last = s // HALO - 1
    return pl.pallas_call(
        functools.partial(_layer1_kernel, seq_len=s),
        grid=(b, s // tm),
        in_specs=[
            pl.BlockSpec((1, tm, d), lambda bi, i: (bi, i, 0)),
            pl.BlockSpec((1, HALO, d), lambda bi, i: (bi, jnp.maximum(i * per - 1, 0), 0)),
            pl.BlockSpec((1, HALO, d), lambda bi, i: (bi, jnp.minimum((i + 1) * per, last), 0)),
            _mod_spec(d, layer, lambda bi, i: bi),
            _layer_spec((1, d), layer),
            _layer_spec((1, d), layer),
            _const_spec((1, d)),
            _const_spec(band.shape),
            _const_spec(wp.shape),
            _layer_spec((d, dff), layer),
            _layer_spec((d, dff), layer),
            _layer_spec((dff, d), layer),
            _const_spec((1, d)),
        ],
        out_specs=pl.BlockSpec((1, tm, d), lambda bi, i: (bi, i, 0)),
        out_shape=jax.ShapeDtypeStruct((b, s, d), F32),
        compiler_params=_params(2),
        name="layer1",
    )(x, x, x, mods, g_mix1, g_ffn1, pool_scale, band, wp, w1, w3, w2, g_final)


def _rope_tables(n):
    rows = n // GRID_W
    row = np.repeat(np.arange(rows), GRID_W).astype(np.float64)
    col = np.tile(np.arange(GRID_W), rows).astype(np.float64)
    half = HEAD_DIM // 2
    freqs = ROPE_THETA ** (-np.arange(0, half, 2, dtype=np.float64) / half)
    ang = np.concatenate([row[:, None] * freqs, col[:, None] * freqs], axis=-1)
    cos = np.tile(np.repeat(np.cos(ang), 2, axis=1), (1, LANES // HEAD_DIM))
    sin = np.tile(np.repeat(np.sin(ang), 2, axis=1), (1, LANES // HEAD_DIM))
    sign = np.where(np.arange(LANES) % 2 == 0, -1.0, 1.0)
    return jnp.asarray(cos, dtype=F32), jnp.asarray(sin * sign, dtype=F32)


def _band_matrices(sub):
    t = np.arange(sub)[:, None]
    e = np.arange(sub + 2 * HALO)[None, :]
    mats = []
    for w in POOL_WINDOWS:
        left = w // 2
        right = w - 1 - left
        member = ((e >= t + HALO - left) & (e <= t + HALO + right)).astype(np.float32)
        mats.append(np.concatenate([member, member], axis=1))
    return jnp.asarray(np.stack(mats), dtype=BF16)


def kernel(x, c, ctx, c_ctx, w_ada, b_ada, g_mix, g_ffn, w_in, w_out, q_norm, k_norm, gmlp_norm,
           w_spatial, b_spatial, w_pool, pool_scale, w1, w3, w2, g_final):
    b, s, d = x.shape
    depth = w_ada.shape[0]
    assert depth == 2 and d == D_MODEL and s % CHUNK == 0
    tm_pre, tq, tm_ffn = 1024, 512, 512

    cond = jnp.concatenate([c, c_ctx[None], jnp.zeros((COND_ROWS - b - 1, d), F32)], axis=0)
    mods = _adaln(cond, w_ada, b_ada).reshape(depth, COND_ROWS, 6, d)

    wi = w_in[0]
    wq = wi[:, :ATTN_WIDTH].reshape(d, N_KV_HEADS, GQA_GROUP, HEAD_DIM).transpose(0, 2, 1, 3)
    w_in_b = jnp.concatenate([wq.reshape(d, ATTN_WIDTH), wi[:, ATTN_WIDTH:]], axis=1).astype(BF16)
    w_kv_b = wi[:, ATTN_WIDTH:ATTN_WIDTH + 2 * KV_WIDTH].astype(BF16)
    wo = w_out[0]
    woa = wo[:ATTN_WIDTH].reshape(N_KV_HEADS, GQA_GROUP, HEAD_DIM, d).transpose(1, 0, 2, 3)
    woa = woa.reshape(ATTN_WIDTH, d).astype(BF16)
    wog = wo[ATTN_WIDTH:].astype(BF16)
    cos_t, sin_t = _rope_tables(s)
    qg = jnp.tile(q_norm[0], LANES // HEAD_DIM)[None]
    kg = jnp.tile(k_norm[0], LANES // HEAD_DIM)[None]
    gv = gmlp_norm[0].reshape(1, GMLP_WIDTH)
    seg = np.arange(2 * LANES) // HEAD_DIM
    bsum = jnp.asarray((seg[:, None] == seg[None, :]).astype(np.float32), dtype=BF16)
    ws = w_spatial[0]
    ws2 = jnp.concatenate([ws[0::2], ws[1::2]], axis=2).astype(BF16)
    bs_t = jnp.repeat(b_spatial[0].T, GMLP_GROUP_DIM, axis=1)

    g_mix3, g_ffn3 = g_mix.reshape(depth, 1, d), g_ffn.reshape(depth, 1, d)

    q, kt, v, gm = _pre0(x, mods, g_mix3, w_in_b, cos_t, sin_t, qg, kg, gv, bsum, ws2, bs_t, tm_pre)
    kct, vc = _ctx_kv(ctx, mods, g_mix3, w_kv_b, kg, bsum, b, CTX_BATCHES_PER_STEP)
    attn, (w1b, w3b, w2b) = _attention(q, kt, v, kct, vc, qg, kg, (w1, w3, w2), tq)
    x1 = _post0(x, attn, gm, mods, g_ffn3, woa, wog, w1b, w3b, w2b, 0, tm_ffn)

    band = _band_matrices(POOL_SUB)
    return _layer1(x1, mods, g_mix3, g_ffn3, pool_scale[:1], band,
                   w_pool[0].astype(BF16), w1b, w3b, w2b, g_final[None], 1, tm_ffn)
```

```python
import functools

import numpy as np
import jax
import jax.numpy as jnp
from jax import lax
from jax.experimental import pallas as pl
from jax.experimental.pallas import tpu as pltpu

D_MODEL = 1024
GRID_W = 64
N_HEADS = 8
N_KV_HEADS = 2
HEAD_DIM = 64
GQA_GROUP = N_HEADS // N_KV_HEADS
ATTN_WIDTH = N_HEADS * HEAD_DIM
KV_WIDTH = N_KV_HEADS * HEAD_DIM
ROPE_THETA = 10000.0
GMLP_GROUPS = 8
GMLP_GROUP_DIM = 64
GMLP_WIDTH = GMLP_GROUPS * GMLP_GROUP_DIM
CHUNK = 128
POOL_WINDOWS = (2, 4, 8, 16)
POOL_GROUP_DIM = D_MODEL // len(POOL_WINDOWS)
EPS = 1e-6
Q_SCALE = float(HEAD_DIM ** -0.5 * np.log2(np.e))

LANES = 128
HALO = 8
COND_ROWS = 16
VMEM_LIMIT = 56 * 1024 * 1024
POOL_SUB = 64
CTX_BATCHES_PER_STEP = 4
KEY_TILE = 256
FFN_CHUNK = 768
MAX_SAFE_SHIFT = 60.0

F32 = jnp.float32
BF16 = jnp.bfloat16


def _const_spec(shape):
    nd = len(shape)
    return pl.BlockSpec(shape, lambda *_: (0,) * nd, pipeline_mode=pl.Buffered(1))


def _layer_spec(shape, layer):
    nd = len(shape)
    return pl.BlockSpec((None,) + tuple(shape), lambda *_: (layer,) + (0,) * nd,
                        pipeline_mode=pl.Buffered(1))


def _mod_spec(d, layer, row):
    return pl.BlockSpec((None, 1, 6, d), lambda *idx: (layer, row(*idx), 0, 0))


def _params(n_axes):
    return pltpu.CompilerParams(dimension_semantics=("arbitrary",) * n_axes,
                                vmem_limit_bytes=VMEM_LIMIT)


def _rms_rows(x, gain):
    ms = jnp.mean(x * x, axis=-1, keepdims=True)
    return (x * lax.rsqrt(ms + EPS)) * gain


def _norm_modulate(x, gain, shift, scale):
    return _rms_rows(x, gain * (1.0 + scale)) + shift


def _seg_mean_sq(ta, tb, bsum_ref):
    sq = jnp.concatenate([(ta * ta).astype(BF16), (tb * tb).astype(BF16)], axis=1)
    ss = jnp.dot(sq, bsum_ref[...], preferred_element_type=F32) * (1.0 / HEAD_DIM)
    return ss[:, :LANES], ss[:, LANES:]


def _swiglu(h, w1_ref, w3_ref, w2_ref):
    dff = w1_ref.shape[1]
    f = None
    for c0 in range(0, dff, FFN_CHUNK):
        c1 = min(c0 + FFN_CHUNK, dff)
        a = jnp.dot(h, w1_ref[:, c0:c1], preferred_element_type=F32)
        b = jnp.dot(h, w3_ref[:, c0:c1], preferred_element_type=F32)
        g = (a * jax.nn.sigmoid(a) * b).astype(BF16)
        d = jnp.dot(g, w2_ref[c0:c1, :], preferred_element_type=F32)
        f = d if f is None else f + d
    return f


def _adaln_kernel(cond_ref, w_ref, b_ref, o_ref):
    s = cond_ref[...]
    s = (s * jax.nn.sigmoid(s)).astype(BF16)
    o_ref[0] = jnp.dot(s, w_ref[0].astype(BF16), preferred_element_type=F32) + b_ref[0]


def _adaln(cond, w_ada, b_ada, tn=1536):
    depth, d, n = w_ada.shape
    return pl.pallas_call(
        _adaln_kernel,
        grid=(depth, n // tn),
        in_specs=[
            pl.BlockSpec((COND_ROWS, d), lambda l, j: (0, 0)),
            pl.BlockSpec((1, d, tn), lambda l, j: (l, 0, j)),
            pl.BlockSpec((1, 1, tn), lambda l, j: (l, 0, j)),
        ],
        out_specs=pl.BlockSpec((1, COND_ROWS, tn), lambda l, j: (l, 0, j)),
        out_shape=jax.ShapeDtypeStruct((depth, COND_ROWS, n), F32),
        compiler_params=_params(2),
        name="adaln",
    )(cond, w_ada, b_ada.reshape(depth, 1, n))


def _rope(t, cos, sin_signed, even_lane):
    partner = jnp.where(even_lane, pltpu.roll(t, LANES - 1, 1), pltpu.roll(t, 1, 1))
    return t * cos + partner * sin_signed


def _gelu(x):
    return 0.5 * x * (1.0 + lax.erf(x * np.float32(np.sqrt(0.5))))


def _store_values_with_ones(v, ve_ref, j=0):
    ve_ref[j, :, :LANES] = v.astype(BF16)
    ve_ref[j, :, LANES:] = jnp.ones(v.shape, BF16)


def _pre0_kernel(x_ref, mod_ref, gmix_ref, win_ref, cos_ref, sin_ref, qg_ref, kg_ref, gv_ref,
                 bsum_ref, ws_ref, bs_ref, q_ref, kt_ref, v_ref, gm_ref):
    tm = x_ref.shape[1]
    m = mod_ref[0]
    h = _norm_modulate(x_ref[0], gmix_ref[...], m[0:1], m[1:2])
    proj = jnp.dot(h.astype(BF16), win_ref[...], preferred_element_type=F32)

    lane = lax.broadcasted_iota(jnp.int32, (tm, LANES), 1)
    even_lane = (lane % 2) == 0
    cos = cos_ref[...]
    sin_signed = sin_ref[...]

    def head_norm_rope(t, mean_sq, gain):
        return _rope((t * lax.rsqrt(mean_sq + EPS)) * gain, cos, sin_signed, even_lane)

    u0 = ATTN_WIDTH + 2 * KV_WIDTH
    g0 = u0 + GMLP_WIDTH
    n_gm = GMLP_WIDTH // LANES
    blocks = [proj[:, g * LANES:(g + 1) * LANES] for g in range(GQA_GROUP)]
    blocks.append(proj[:, ATTN_WIDTH:ATTN_WIDTH + KV_WIDTH])
    blocks += [_gelu(proj[:, g0 + j * LANES:g0 + (j + 1) * LANES]) for j in range(n_gm)]
    mean_sq = []
    for a in range(0, len(blocks), 2):
        pair = _seg_mean_sq(blocks[a], blocks[min(a + 1, len(blocks) - 1)], bsum_ref)
        mean_sq += list(pair)

    for g in range(GQA_GROUP):
        q_ref[0, :, g * LANES:(g + 1) * LANES] = (
            head_norm_rope(blocks[g], mean_sq[g], qg_ref[...]) * Q_SCALE).astype(BF16)
    k = head_norm_rope(blocks[GQA_GROUP], mean_sq[GQA_GROUP], kg_ref[...])
    kt_ref[0] = k.T.astype(BF16)
    _store_values_with_ones(proj[:, ATTN_WIDTH + KV_WIDTH:ATTN_WIDTH + 2 * KV_WIDTH], v_ref)

    left = lax.broadcasted_iota(jnp.int32, (CHUNK, LANES), 1) < GMLP_GROUP_DIM
    for j in range(n_gm):
        u = _gelu(proj[:, u0 + j * LANES:u0 + (j + 1) * LANES])
        vv = blocks[GQA_GROUP + 1 + j]
        vg = (vv * lax.rsqrt(mean_sq[GQA_GROUP + 1 + j] + EPS)) * gv_ref[:, j * LANES:(j + 1) * LANES]
        bias = bs_ref[:, j * LANES:(j + 1) * LANES]
        for n in range(tm // CHUNK):
            blk = vg[n * CHUNK:(n + 1) * CHUNK]
            rhs = jnp.concatenate([jnp.where(left, blk, 0.0), jnp.where(left, 0.0, blk)],
                                  axis=0).astype(BF16)
            mixed = jnp.dot(ws_ref[j], rhs, preferred_element_type=F32) + bias
            gm_ref[0, n * CHUNK:(n + 1) * CHUNK, j * LANES:(j + 1) * LANES] = (
                u[n * CHUNK:(n + 1) * CHUNK] * mixed).astype(BF16)


def _pre0(x, mods, g_mix0, w_in_b, cos_t, sin_t, qg, kg, gv, bsum, ws2, bs_t, tm):
    b, s, d = x.shape
    nw = w_in_b.shape[1]
    return pl.pallas_call(
        _pre0_kernel,
        grid=(b, s // tm),
        in_specs=[
            pl.BlockSpec((1, tm, d), lambda bi, i: (bi, i, 0)),
            _mod_spec(d, 0, lambda bi, i: bi),
            _layer_spec((1, d), 0),
            _const_spec((d, nw)),
            pl.BlockSpec((tm, LANES), lambda bi, i: (i, 0)),
            pl.BlockSpec((tm, LANES), lambda bi, i: (i, 0)),
            _const_spec((1, LANES)),
            _const_spec((1, LANES)),
            _const_spec((1, GMLP_WIDTH)),
            _const_spec((2 * LANES, 2 * LANES)),
            _const_spec(ws2.shape),
            _const_spec(bs_t.shape),
        ],
        out_specs=[
            pl.BlockSpec((1, tm, ATTN_WIDTH), lambda bi, i: (bi, i, 0)),
            pl.BlockSpec((1, KV_WIDTH, tm), lambda bi, i: (bi, 0, i)),
            pl.BlockSpec((1, tm, 2 * LANES), lambda bi, i: (bi, i, 0)),
            pl.BlockSpec((1, tm, GMLP_WIDTH), lambda bi, i: (bi, i, 0)),
        ],
        out_shape=[
            jax.ShapeDtypeStruct((b, s, ATTN_WIDTH), BF16),
            jax.ShapeDtypeStruct((b, KV_WIDTH, s), BF16),
            jax.ShapeDtypeStruct((b, s, 2 * LANES), BF16),
            jax.ShapeDtypeStruct((b, s, GMLP_WIDTH), BF16),
        ],
        compiler_params=_params(2),
        name="pre0",
    )(x, mods, g_mix0, w_in_b, cos_t, sin_t, qg, kg, gv, bsum, ws2, bs_t)


def _ctx_kernel(c_ref, mod_ref, gmix_ref, wkv_ref, kg_ref, bsum_ref, kt_ref, v_ref):
    bb, n, d = c_ref.shape
    m = mod_ref[0]
    h = _norm_modulate(c_ref[...].reshape(bb * n, d), gmix_ref[...], m[0:1], m[1:2])
    proj = jnp.dot(h.astype(BF16), wkv_ref[...], preferred_element_type=F32)
    k = proj[:, :KV_WIDTH]
    k = (k * lax.rsqrt(_seg_mean_sq(k, k, bsum_ref)[0] + EPS)) * kg_ref[...]
    for j in range(bb):
        kt_ref[j] = k[j * n:(j + 1) * n].T.astype(BF16)
        _store_values_with_ones(proj[j * n:(j + 1) * n, KV_WIDTH:], v_ref, j)


def _ctx_kv(ctx, mods, g_mix0, w_kv_b, kg, bsum, ctx_row, bb):
    b, n, d = ctx.shape
    assert b % bb == 0
    return pl.pallas_call(
        _ctx_kernel,
        grid=(b // bb,),
        in_specs=[
            pl.BlockSpec((bb, n, d), lambda bi: (bi, 0, 0)),
            _mod_spec(d, 0, lambda bi: ctx_row),
            _layer_spec((1, d), 0),
            _const_spec((d, 2 * KV_WIDTH)),
            _const_spec((1, LANES)),
            _const_spec((2 * LANES, 2 * LANES)),
        ],
        out_specs=[
            pl.BlockSpec((bb, KV_WIDTH, n), lambda bi: (bi, 0, 0)),
            pl.BlockSpec((bb, n, 2 * LANES), lambda bi: (bi, 0, 0)),
        ],
        out_shape=[
            jax.ShapeDtypeStruct((b, KV_WIDTH, n), BF16),
            jax.ShapeDtypeStruct((b, n, 2 * LANES), BF16),
        ],
        compiler_params=_params(1),
        name="ctx_kv",
    )(ctx, mods, g_mix0, w_kv_b, kg, bsum)


def _attn_kernel(q_ref, kt_ref, ve_ref, kct_ref, vce_ref, qg_ref, kg_ref, w1f_ref, w3f_ref, w2f_ref,
                 o_ref, w1b_ref, w3b_ref, w2b_ref, qz_ref, r_ref):
    tq = q_ref.shape[1]
    n_heads = GQA_GROUP * N_KV_HEADS
    left = lax.broadcasted_iota(jnp.int32, (tq, LANES), 1) < HEAD_DIM
    m = (Q_SCALE * HEAD_DIM) * jnp.max(jnp.abs(qg_ref[...])) * jnp.max(jnp.abs(kg_ref[...]))

    def masked_queries(g):
        qc = q_ref[0, :, g * LANES:(g + 1) * LANES].astype(F32)
        return jnp.where(left, qc, 0.0).astype(BF16), jnp.where(left, 0.0, qc).astype(BF16)

    def normalised(r):
        return r[:, :LANES] / r[:, LANES:]

    def store_group(g, o_kh0, o_kh1):
        o_ref[0, :, g * LANES:(g + 1) * LANES] = jnp.where(left, o_kh0, o_kh1).astype(BF16)

    def key_tiles():
        for k_ref, v_ref in ((kt_ref, ve_ref), (kct_ref, vce_ref)):
            for j in range(k_ref.shape[2] // KEY_TILE):
                yield (k_ref.at[0, :, j * KEY_TILE:(j + 1) * KEY_TILE],
                       v_ref.at[0, j * KEY_TILE:(j + 1) * KEY_TILE, :])

    def cast_weight_slabs():
        for src, dst in ((w1f_ref, w1b_ref), (w3f_ref, w3b_ref), (w2f_ref, w2b_ref)):
            dst[...] = src[...].astype(BF16)

    def streamed():
        cast_weight_slabs()
        for g in range(GQA_GROUP):
            outs = []
            for qz in masked_queries(g):
                r = None
                for k_tile, v_tile in key_tiles():
                    s = jnp.dot(qz, k_tile[...], preferred_element_type=F32)
                    p = jnp.exp2(s - m).astype(BF16)
                    d = jnp.dot(p, v_tile[...], preferred_element_type=F32)
                    r = d if r is None else r + d
                outs.append(normalised(r))
            store_group(g, *outs)

    def exact_max():
        cast_weight_slabs()
        for g in range(GQA_GROUP):
            qz_ref[N_KV_HEADS * g], qz_ref[N_KV_HEADS * g + 1] = masked_queries(g)

        def head(u, carry):
            s1 = jnp.dot(qz_ref[u], kt_ref[0], preferred_element_type=F32)
            s2 = jnp.dot(qz_ref[u], kct_ref[0], preferred_element_type=F32)
            mx = jnp.maximum(jnp.max(s1, axis=-1, keepdims=True), jnp.max(s2, axis=-1, keepdims=True))
            r_ref[u] = normalised(
                jnp.dot(jnp.exp2(s1 - mx).astype(BF16), ve_ref[0], preferred_element_type=F32)
                + jnp.dot(jnp.exp2(s2 - mx).astype(BF16), vce_ref[0], preferred_element_type=F32))
            return carry
        lax.fori_loop(0, n_heads, head, 0)
        for g in range(GQA_GROUP):
            store_group(g, r_ref[N_KV_HEADS * g], r_ref[N_KV_HEADS * g + 1])

    lax.cond(m <= MAX_SAFE_SHIFT, streamed, exact_max)


def _attention(q, kt, ve, kct, vce, qg, kg, ffn_weights, tq):
    b, s, _ = q.shape
    nc = kct.shape[2]
    n_heads = GQA_GROUP * N_KV_HEADS
    assert s % KEY_TILE == 0 and nc % KEY_TILE == 0
    steps = b * (s // tq)
    per_b = s // tq
    flat = [w.reshape(-1, w.shape[-1]) for w in ffn_weights]
    slab_specs = []
    for w in flat:
        every = next(e for e in (1, 2, 4, 8) if steps % e == 0 and w.shape[0] % (steps // e * 16) == 0)
        slab_specs.append(pl.BlockSpec((w.shape[0] // (steps // every), w.shape[1]),
                                       lambda bi, i, every=every: ((bi * per_b + i) // every, 0)))
    out = pl.pallas_call(
        _attn_kernel,
        grid=(b, s // tq),
        in_specs=[
            pl.BlockSpec((1, tq, ATTN_WIDTH), lambda bi, i: (bi, i, 0)),
            pl.BlockSpec((1, KV_WIDTH, s), lambda bi, i: (bi, 0, 0)),
            pl.BlockSpec((1, s, 2 * LANES), lambda bi, i: (bi, 0, 0)),
            pl.BlockSpec((1, KV_WIDTH, nc), lambda bi, i: (bi, 0, 0)),
            pl.BlockSpec((1, nc, 2 * LANES), lambda bi, i: (bi, 0, 0)),
            _const_spec((1, LANES)),
            _const_spec((1, LANES)),
        ] + slab_specs,
        out_specs=[pl.BlockSpec((1, tq, ATTN_WIDTH), lambda bi, i: (bi, i, 0))] + slab_specs,
        out_shape=[jax.ShapeDtypeStruct((b, s, ATTN_WIDTH), BF16)]
        + [jax.ShapeDtypeStruct(w.shape, BF16) for w in flat],
        scratch_shapes=[
            pltpu.VMEM((n_heads, tq, LANES), BF16),
            pltpu.VMEM((n_heads, tq, LANES), F32),
        ],
        compiler_params=_params(2),
        name="attention",
    )(q, kt, ve, kct, vce, qg, kg, *flat)
    return out[0], [wb.reshape(w.shape) for wb, w in zip(out[1:], ffn_weights)]


def _post0_kernel(x_ref, a_ref, gm_ref, mod_ref, gffn_ref, woa_ref, wog_ref, w1_ref, w3_ref, w2_ref,
                  o_ref):
    m = mod_ref[0]
    mix = (jnp.dot(a_ref[0], woa_ref[...], preferred_element_type=F32)
           + jnp.dot(gm_ref[0], wog_ref[...], preferred_element_type=F32))
    x1 = x_ref[0] + m[2:3] * mix
    h = _norm_modulate(x1, gffn_ref[...], m[3:4], m[4:5])
    o_ref[0] = x1 + m[5:6] * _swiglu(h.astype(BF16), w1_ref, w3_ref, w2_ref)


def _post0(x, attn, gm, mods, g_ffn0, woa, wog, w1, w3, w2, layer, tm):
    b, s, d = x.shape
    dff = w1.shape[2]
    return pl.pallas_call(
        _post0_kernel,
        grid=(b, s // tm),
        in_specs=[
            pl.BlockSpec((1, tm, d), lambda bi, i: (bi, i, 0)),
            pl.BlockSpec((1, tm, ATTN_WIDTH), lambda bi, i: (bi, i, 0)),
            pl.BlockSpec((1, tm, GMLP_WIDTH), lambda bi, i: (bi, i, 0)),
            _mod_spec(d, layer, lambda bi, i: bi),
            _layer_spec((1, d), layer),
            _const_spec((ATTN_WIDTH, d)),
            _const_spec((GMLP_WIDTH, d)),
            _layer_spec((d, dff), layer),
            _layer_spec((d, dff), layer),
            _layer_spec((dff, d), layer),
        ],
        out_specs=pl.BlockSpec((1, tm, d), lambda bi, i: (bi, i, 0)),
        out_shape=jax.ShapeDtypeStruct((b, s, d), F32),
        compiler_params=_params(2),
        name="post0",
    )(x, attn, gm, mods, g_ffn0, woa, wog, w1, w3, w2)


def _layer1_kernel(x_ref, prev_ref, next_ref, mod_ref, gmix_ref, gffn_ref, ps_ref, band_ref, wp_ref,
                   w1_ref, w3_ref, w2_ref, gfin_ref, o_ref, *, seq_len):
    tm = x_ref.shape[1]
    i = pl.program_id(1)
    m = mod_ref[0]

    def norm_mod(t):
        return _norm_modulate(t, gmix_ref[...], m[0:1], m[1:2])

    x = x_ref[0]
    xn = norm_mod(x)
    xp = jnp.where(i > 0, norm_mod(prev_ref[0]), 0.0)
    xq = jnp.where(i < pl.num_programs(1) - 1, norm_mod(next_ref[0]), 0.0)
    ext = jnp.concatenate([xp, xn, xq], axis=0)
    ext_hi = ext.astype(BF16)
    ext_lo = (ext - ext_hi.astype(F32)).astype(BF16)

    sub = band_ref.shape[1]
    pos = i * tm + lax.broadcasted_iota(jnp.int32, (tm, LANES), 0)
    ys = []
    for gi, w in enumerate(POOL_WINDOWS):
        left_w = w // 2
        right_w = w - 1 - left_w
        cnt = (jnp.minimum(pos + right_w + 1, seq_len) - jnp.maximum(pos - left_w, 0)).astype(F32)
        inv_cnt = jnp.concatenate([1.0 / cnt] * (POOL_GROUP_DIM // LANES), axis=1)
        sl = slice(gi * POOL_GROUP_DIM, (gi + 1) * POOL_GROUP_DIM)
        sums = []
        for r in range(tm // sub):
            rows = slice(r * sub, r * sub + sub + 2 * HALO)
            hi_lo = jnp.concatenate([ext_hi[rows, sl], ext_lo[rows, sl]], axis=0)
            sums.append(jnp.dot(band_ref[gi], hi_lo, preferred_element_type=F32))
        pooled = jnp.concatenate(sums, axis=0) * inv_cnt - xn[:, sl]
        ys.append(jnp.dot(pooled.astype(BF16), wp_ref[gi], preferred_element_type=F32))
    y = jnp.concatenate(ys, axis=1) * ps_ref[...]
    x1 = x + m[2:3] * y
    h = _norm_modulate(x1, gffn_ref[...], m[3:4], m[4:5])
    x2 = x1 + m[5:6] * _swiglu(h.astype(BF16), w1_ref, w3_ref, w2_ref)
    o_ref[0] = _rms_rows(x2, gfin_ref[...])


def _layer1(x, mods, g_mix1, g_ffn1, pool_scale, band, wp, w1, w3, w2, g_final, layer, tm):
    b, s, d = x.shape
    dff = w1.shape[2]
    per = tm // HALO
    last = s // HALO - 1
    return pl.pallas_call(
        functools.partial(_layer1_kernel, seq_len=s),
        grid=(b, s // tm),
        in_specs=[
            pl.BlockSpec((1, tm, d), lambda bi, i: (bi, i, 0)),
            pl.BlockSpec((1, HALO, d), lambda bi, i: (bi, jnp.maximum(i * per - 1, 0), 0)),
            pl.BlockSpec((1, HALO, d), lambda bi, i: (bi, jnp.minimum((i + 1) * per, last), 0)),
            _mod_spec(d, layer, lambda bi, i: bi),
            _layer_spec((1, d), layer),
            _layer_spec((1, d), layer),
            _const_spec((1, d)),
            _const_spec(band.shape),
            _const_spec(wp.shape),
            _layer_spec((d, dff), layer),
            _layer_spec((d, dff), layer),
            _layer_spec((dff, d), layer),
            _const_spec((1, d)),
        ],
        out_specs=pl.BlockSpec((1, tm, d), lambda bi, i: (bi, i, 0)),
        out_shape=jax.ShapeDtypeStruct((b, s, d), F32),
        compiler_params=_params(2),
        name="layer1",
    )(x, x, x, mods, g_mix1, g_ffn1, pool_scale, band, wp, w1, w3, w2, g_final)


def _rope_tables(n):
    rows = n // GRID_W
    row = np.repeat(np.arange(rows), GRID_W).astype(np.float64)
    col = np.tile(np.arange(GRID_W), rows).astype(np.float64)
    half = HEAD_DIM // 2
    freqs = ROPE_THETA ** (-np.arange(0, half, 2, dtype=np.float64) / half)
    ang = np.concatenate([row[:, None] * freqs, col[:, None] * freqs], axis=-1)
    cos = np.tile(np.repeat(np.cos(ang), 2, axis=1), (1, LANES // HEAD_DIM))
    sin = np.tile(np.repeat(np.sin(ang), 2, axis=1), (1, LANES // HEAD_DIM))
    sign = np.where(np.arange(LANES) % 2 == 0, -1.0, 1.0)
    return jnp.asarray(cos, dtype=F32), jnp.asarray(sin * sign, dtype=F32)


def _band_matrices(sub):
    t = np.arange(sub)[:, None]
    e = np.arange(sub + 2 * HALO)[None, :]
    mats = []
    for w in POOL_WINDOWS:
        left = w // 2
        right = w - 1 - left
        member = ((e >= t + HALO - left) & (e <= t + HALO + right)).astype(np.float32)
        mats.append(np.concatenate([member, member], axis=1))
    return jnp.asarray(np.stack(mats), dtype=BF16)


def kernel(x, c, ctx, c_ctx, w_ada, b_ada, g_mix, g_ffn, w_in, w_out, q_norm, k_norm, gmlp_norm,
           w_spatial, b_spatial, w_pool, pool_scale, w1, w3, w2, g_final):
    b, s, d = x.shape
    depth = w_ada.shape[0]
    assert depth == 2 and d == D_MODEL and s % CHUNK == 0
    tm_pre, tq, tm_ffn = 1024, 512, 1024

    cond = jnp.concatenate([c, c_ctx[None], jnp.zeros((COND_ROWS - b - 1, d), F32)], axis=0)
    mods = _adaln(cond, w_ada, b_ada).reshape(depth, COND_ROWS, 6, d)

    wi = w_in[0]
    wq = wi[:, :ATTN_WIDTH].reshape(d, N_KV_HEADS, GQA_GROUP, HEAD_DIM).transpose(0, 2, 1, 3)
    w_in_b = jnp.concatenate([wq.reshape(d, ATTN_WIDTH), wi[:, ATTN_WIDTH:]], axis=1).astype(BF16)
    w_kv_b = wi[:, ATTN_WIDTH:ATTN_WIDTH + 2 * KV_WIDTH].astype(BF16)
    wo = w_out[0]
    woa = wo[:ATTN_WIDTH].reshape(N_KV_HEADS, GQA_GROUP, HEAD_DIM, d).transpose(1, 0, 2, 3)
    woa = woa.reshape(ATTN_WIDTH, d).astype(BF16)
    wog = wo[ATTN_WIDTH:].astype(BF16)
    cos_t, sin_t = _rope_tables(s)
    qg = jnp.tile(q_norm[0], LANES // HEAD_DIM)[None]
    kg = jnp.tile(k_norm[0], LANES // HEAD_DIM)[None]
    gv = gmlp_norm[0].reshape(1, GMLP_WIDTH)
    seg = np.arange(2 * LANES) // HEAD_DIM
    bsum = jnp.asarray((seg[:, None] == seg[None, :]).astype(np.float32), dtype=BF16)
    ws = w_spatial[0]
    ws2 = jnp.concatenate([ws[0::2], ws[1::2]], axis=2).astype(BF16)
    bs_t = jnp.repeat(b_spatial[0].T, GMLP_GROUP_DIM, axis=1)

    g_mix3, g_ffn3 = g_mix.reshape(depth, 1, d), g_ffn.reshape(depth, 1, d)

    q, kt, v, gm = _pre0(x, mods, g_mix3, w_in_b, cos_t, sin_t, qg, kg, gv, bsum, ws2, bs_t, tm_pre)
    kct, vc = _ctx_kv(ctx, mods, g_mix3, w_kv_b, kg, bsum, b, CTX_BATCHES_PER_STEP)
    attn, (w1b, w3b, w2b) = _attention(q, kt, v, kct, vc, qg, kg, (w1, w3, w2), tq)
    x1 = _post0(x, attn, gm, mods, g_ffn3, woa, wog, w1b, w3b, w2b, 0, tm_ffn)

    band = _band_matrices(POOL_SUB)
    return _layer1(x1, mods, g_mix3, g_ffn3, pool_scale[:1], band,
                   w_pool[0].astype(BF16), w1b, w3b, w2b, g_final[None], 1, tm_ffn)
```

```python
import functools

import numpy as np
import jax
import jax.numpy as jnp
from jax import lax
from jax.experimental import pallas as pl
from jax.experimental.pallas import tpu as pltpu

D_MODEL = 1024
GRID_W = 64
N_HEADS = 8
N_KV_HEADS = 2
HEAD_DIM = 64
GQA_GROUP = N_HEADS // N_KV_HEADS
ATTN_WIDTH = N_HEADS * HEAD_DIM
KV_WIDTH = N_KV_HEADS * HEAD_DIM
ROPE_THETA = 10000.0
GMLP_GROUPS = 8
GMLP_GROUP_DIM = 64
GMLP_WIDTH = GMLP_GROUPS * GMLP_GROUP_DIM
CHUNK = 128
POOL_WINDOWS = (2, 4, 8, 16)
POOL_GROUP_DIM = D_MODEL // len(POOL_WINDOWS)
EPS = 1e-6
Q_SCALE = float(HEAD_DIM ** -0.5 * np.log2(np.e))

LANES = 128
HALO = 8
COND_ROWS = 16
VMEM_LIMIT = 56 * 1024 * 1024
POOL_SUB = 64
CTX_BATCHES_PER_STEP = 4
KEY_TILE = 256
FFN_CHUNK = 768
MAX_SAFE_SHIFT = 60.0

F32 = jnp.float32
BF16 = jnp.bfloat16


def _const_spec(shape):
    nd = len(shape)
    return pl.BlockSpec(shape, lambda *_: (0,) * nd, pipeline_mode=pl.Buffered(1))


def _layer_spec(shape, layer):
    nd = len(shape)
    return pl.BlockSpec((None,) + tuple(shape), lambda *_: (layer,) + (0,) * nd,
                        pipeline_mode=pl.Buffered(1))


def _mod_spec(d, layer, row):
    return pl.BlockSpec((None, 1, 6, d), lambda *idx: (layer, row(*idx), 0, 0))


def _params(n_axes):
    return pltpu.CompilerParams(dimension_semantics=("arbitrary",) * n_axes,
                                vmem_limit_bytes=VMEM_LIMIT)


def _rms_rows(x, gain):
    ms = jnp.mean(x * x, axis=-1, keepdims=True)
    return (x * lax.rsqrt(ms + EPS)) * gain


def _norm_modulate(x, gain, shift, scale):
    return _rms_rows(x, gain * (1.0 + scale)) + shift


def _seg_mean_sq(ta, tb, bsum_ref):
    sq = jnp.concatenate([(ta * ta).astype(BF16), (tb * tb).astype(BF16)], axis=1)
    ss = jnp.dot(sq, bsum_ref[...], preferred_element_type=F32) * (1.0 / HEAD_DIM)
    return ss[:, :LANES], ss[:, LANES:]


def _swiglu(h, w1_ref, w3_ref, w2_ref):
    dff = w1_ref.shape[1]
    f = None
    for c0 in range(0, dff, FFN_CHUNK):
        c1 = min(c0 + FFN_CHUNK, dff)
        a = jnp.dot(h, w1_ref[:, c0:c1], preferred_element_type=F32)
        b = jnp.dot(h, w3_ref[:, c0:c1], preferred_element_type=F32)
        g = (a * jax.nn.sigmoid(a) * b).astype(BF16)
        d = jnp.dot(g, w2_ref[c0:c1, :], preferred_element_type=F32)
        f = d if f is None else f + d
    return f


def _adaln_kernel(cond_ref, w_ref, b_ref, o_ref):
    s = cond_ref[...]
    s = (s * jax.nn.sigmoid(s)).astype(BF16)
    o_ref[0] = jnp.dot(s, w_ref[0].astype(BF16), preferred_element_type=F32) + b_ref[0]


def _adaln(cond, w_ada, b_ada, tn=1536):
    depth, d, n = w_ada.shape
    return pl.pallas_call(
        _adaln_kernel,
        grid=(depth, n // tn),
        in_specs=[
            pl.BlockSpec((COND_ROWS, d), lambda l, j: (0, 0)),
            pl.BlockSpec((1, d, tn), lambda l, j: (l, 0, j)),
            pl.BlockSpec((1, 1, tn), lambda l, j: (l, 0, j)),
        ],
        out_specs=pl.BlockSpec((1, COND_ROWS, tn), lambda l, j: (l, 0, j)),
        out_shape=jax.ShapeDtypeStruct((depth, COND_ROWS, n), F32),
        compiler_params=_params(2),
        name="adaln",
    )(cond, w_ada, b_ada.reshape(depth, 1, n))


def _rope(t, cos, sin_signed, even_lane):
    partner = jnp.where(even_lane, pltpu.roll(t, LANES - 1, 1), pltpu.roll(t, 1, 1))
    return t * cos + partner * sin_signed


def _gelu(x):
    return 0.5 * x * (1.0 + lax.erf(x * np.float32(np.sqrt(0.5))))


def _store_values_with_ones(v, ve_ref, j=0):
    ve_ref[j, :, :LANES] = v.astype(BF16)
    ve_ref[j, :, LANES:] = jnp.ones(v.shape, BF16)


def _pre0_kernel(x_ref, mod_ref, gmix_ref, win_ref, cos_ref, sin_ref, qg_ref, kg_ref, gv_ref,
                 bsum_ref, ws_ref, bs_ref, q_ref, kt_ref, v_ref, gm_ref):
    tm = x_ref.shape[1]
    m = mod_ref[0]
    h = _norm_modulate(x_ref[0], gmix_ref[...], m[0:1], m[1:2])
    hb = h.astype(BF16)

    def project(c0, c1):
        return jnp.dot(hb, win_ref[:, c0:c1], preferred_element_type=F32)

    lane = lax.broadcasted_iota(jnp.int32, (tm, LANES), 1)
    even_lane = (lane % 2) == 0
    cos = cos_ref[...]
    sin_signed = sin_ref[...]

    def head_norm_rope(t, mean_sq, gain):
        return _rope((t * lax.rsqrt(mean_sq + EPS)) * gain, cos, sin_signed, even_lane)

    u0 = ATTN_WIDTH + 2 * KV_WIDTH
    g0 = u0 + GMLP_WIDTH
    n_gm = GMLP_WIDTH // LANES
    pq = project(0, ATTN_WIDTH)
    pkv = project(ATTN_WIDTH, u0)
    pgv = project(g0, g0 + GMLP_WIDTH)
    blocks = [pq[:, g * LANES:(g + 1) * LANES] for g in range(GQA_GROUP)]
    blocks.append(pkv[:, :KV_WIDTH])
    blocks += [_gelu(pgv[:, j * LANES:(j + 1) * LANES]) for j in range(n_gm)]
    mean_sq = []
    for a in range(0, len(blocks), 2):
        pair = _seg_mean_sq(blocks[a], blocks[min(a + 1, len(blocks) - 1)], bsum_ref)
        mean_sq += list(pair)

    for g in range(GQA_GROUP):
        q_ref[0, :, g * LANES:(g + 1) * LANES] = (
            head_norm_rope(blocks[g], mean_sq[g], qg_ref[...]) * Q_SCALE).astype(BF16)
    k = head_norm_rope(blocks[GQA_GROUP], mean_sq[GQA_GROUP], kg_ref[...])
    kt_ref[0] = k.T.astype(BF16)
    _store_values_with_ones(pkv[:, KV_WIDTH:], v_ref)
    pu = project(u0, g0)

    left = lax.broadcasted_iota(jnp.int32, (CHUNK, LANES), 1) < GMLP_GROUP_DIM
    for j in range(n_gm):
        u = _gelu(pu[:, j * LANES:(j + 1) * LANES])
        vv = blocks[GQA_GROUP + 1 + j]
        vg = (vv * lax.rsqrt(mean_sq[GQA_GROUP + 1 + j] + EPS)) * gv_ref[:, j * LANES:(j + 1) * LANES]
        bias = bs_ref[:, j * LANES:(j + 1) * LANES]
        for n in range(tm // CHUNK):
            blk = vg[n * CHUNK:(n + 1) * CHUNK]
            rhs = jnp.concatenate([jnp.where(left, blk, 0.0), jnp.where(left, 0.0, blk)],
                                  axis=0).astype(BF16)
            mixed = jnp.dot(ws_ref[j], rhs, preferred_element_type=F32) + bias
            gm_ref[0, n * CHUNK:(n + 1) * CHUNK, j * LANES:(j + 1) * LANES] = (
                u[n * CHUNK:(n + 1) * CHUNK] * mixed).astype(BF16)


def _pre0(x, mods, g_mix0, w_in_b, cos_t, sin_t, qg, kg, gv, bsum, ws2, bs_t, tm):
    b, s, d = x.shape
    nw = w_in_b.shape[1]
    return pl.pallas_call(
        _pre0_kernel,
        grid=(b, s // tm),
        in_specs=[
            pl.BlockSpec((1, tm, d), lambda bi, i: (bi, i, 0)),
            _mod_spec(d, 0, lambda bi, i: bi),
            _layer_spec((1, d), 0),
            _const_spec((d, nw)),
            pl.BlockSpec((tm, LANES), lambda bi, i: (i, 0)),
            pl.BlockSpec((tm, LANES), lambda bi, i: (i, 0)),
            _const_spec((1, LANES)),
            _const_spec((1, LANES)),
            _const_spec((1, GMLP_WIDTH)),
            _const_spec((2 * LANES, 2 * LANES)),
            _const_spec(ws2.shape),
            _const_spec(bs_t.shape),
        ],
        out_specs=[
            pl.BlockSpec((1, tm, ATTN_WIDTH), lambda bi, i: (bi, i, 0)),
            pl.BlockSpec((1, KV_WIDTH, tm), lambda bi, i: (bi, 0, i)),
            pl.BlockSpec((1, tm, 2 * LANES), lambda bi, i: (bi, i, 0)),
            pl.BlockSpec((1, tm, GMLP_WIDTH), lambda bi, i: (bi, i, 0)),
        ],
        out_shape=[
            jax.ShapeDtypeStruct((b, s, ATTN_WIDTH), BF16),
            jax.ShapeDtypeStruct((b, KV_WIDTH, s), BF16),
            jax.ShapeDtypeStruct((b, s, 2 * LANES), BF16),
            jax.ShapeDtypeStruct((b, s, GMLP_WIDTH), BF16),
        ],
        compiler_params=_params(2),
        name="pre0",
    )(x, mods, g_mix0, w_in_b, cos_t, sin_t, qg, kg, gv, bsum, ws2, bs_t)


def _ctx_kernel(c_ref, mod_ref, gmix_ref, wkv_ref, kg_ref, bsum_ref, kt_ref, v_ref):
    bb, n, d = c_ref.shape
    m = mod_ref[0]
    h = _norm_modulate(c_ref[...].reshape(bb * n, d), gmix_ref[...], m[0:1], m[1:2])
    proj = jnp.dot(h.astype(BF16), wkv_ref[...], preferred_element_type=F32)
    k = proj[:, :KV_WIDTH]
    k = (k * lax.rsqrt(_seg_mean_sq(k, k, bsum_ref)[0] + EPS)) * kg_ref[...]
    for j in range(bb):
        kt_ref[j] = k[j * n:(j + 1) * n].T.astype(BF16)
        _store_values_with_ones(proj[j * n:(j + 1) * n, KV_WIDTH:], v_ref, j)


def _ctx_kv(ctx, mods, g_mix0, w_kv_b, kg, bsum, ctx_row, bb):
    b, n, d = ctx.shape
    assert b % bb == 0
    return pl.pallas_call(
        _ctx_kernel,
        grid=(b // bb,),
        in_specs=[
            pl.BlockSpec((bb, n, d), lambda bi: (bi, 0, 0)),
            _mod_spec(d, 0, lambda bi: ctx_row),
            _layer_spec((1, d), 0),
            _const_spec((d, 2 * KV_WIDTH)),
            _const_spec((1, LANES)),
            _const_spec((2 * LANES, 2 * LANES)),
        ],
        out_specs=[
            pl.BlockSpec((bb, KV_WIDTH, n), lambda bi: (bi, 0, 0)),
            pl.BlockSpec((bb, n, 2 * LANES), lambda bi: (bi, 0, 0)),
        ],
        out_shape=[
            jax.ShapeDtypeStruct((b, KV_WIDTH, n), BF16),
            jax.ShapeDtypeStruct((b, n, 2 * LANES), BF16),
        ],
        compiler_params=_params(1),
        name="ctx_kv",
    )(ctx, mods, g_mix0, w_kv_b, kg, bsum)


def _attn_kernel(q_ref, kt_ref, ve_ref, kct_ref, vce_ref, qg_ref, kg_ref, w1f_ref, w3f_ref, w2f_ref,
                 o_ref, w1b_ref, w3b_ref, w2b_ref, qz_ref, r_ref):
    tq = q_ref.shape[1]
    n_heads = GQA_GROUP * N_KV_HEADS
    left = lax.broadcasted_iota(jnp.int32, (tq, LANES), 1) < HEAD_DIM
    m = (Q_SCALE * HEAD_DIM) * jnp.max(jnp.abs(qg_ref[...])) * jnp.max(jnp.abs(kg_ref[...]))

    def masked_queries(g):
        qc = q_ref[0, :, g * LANES:(g + 1) * LANES].astype(F32)
        return jnp.where(left, qc, 0.0).astype(BF16), jnp.where(left, 0.0, qc).astype(BF16)

    def normalised(r):
        return r[:, :LANES] / r[:, LANES:]

    def store_group(g, o_kh0, o_kh1):
        o_ref[0, :, g * LANES:(g + 1) * LANES] = jnp.where(left, o_kh0, o_kh1).astype(BF16)

    def key_tiles():
        for k_ref, v_ref in ((kt_ref, ve_ref), (kct_ref, vce_ref)):
            for j in range(k_ref.shape[2] // KEY_TILE):
                yield (k_ref.at[0, :, j * KEY_TILE:(j + 1) * KEY_TILE],
                       v_ref.at[0, j * KEY_TILE:(j + 1) * KEY_TILE, :])

    def cast_weight_slabs():
        for src, dst in ((w1f_ref, w1b_ref), (w3f_ref, w3b_ref), (w2f_ref, w2b_ref)):
            dst[...] = src[...].astype(BF16)

    def streamed():
        cast_weight_slabs()
        for g in range(GQA_GROUP):
            outs = []
            for qz in masked_queries(g):
                r = None
                for k_tile, v_tile in key_tiles():
                    s = jnp.dot(qz, k_tile[...], preferred_element_type=F32)
                    p = jnp.exp2(s - m).astype(BF16)
                    d = jnp.dot(p, v_tile[...], preferred_element_type=F32)
                    r = d if r is None else r + d
                outs.append(normalised(r))
            store_group(g, *outs)

    def exact_max():
        cast_weight_slabs()
        for g in range(GQA_GROUP):
            qz_ref[N_KV_HEADS * g], qz_ref[N_KV_HEADS * g + 1] = masked_queries(g)

        def head(u, carry):
            s1 = jnp.dot(qz_ref[u], kt_ref[0], preferred_element_type=F32)
            s2 = jnp.dot(qz_ref[u], kct_ref[0], preferred_element_type=F32)
            mx = jnp.maximum(jnp.max(s1, axis=-1, keepdims=True), jnp.max(s2, axis=-1, keepdims=True))
            r_ref[u] = normalised(
                jnp.dot(jnp.exp2(s1 - mx).astype(BF16), ve_ref[0], preferred_element_type=F32)
                + jnp.dot(jnp.exp2(s2 - mx).astype(BF16), vce_ref[0], preferred_element_type=F32))
            return carry
        lax.fori_loop(0, n_heads, head, 0)
        for g in range(GQA_GROUP):
            store_group(g, r_ref[N_KV_HEADS * g], r_ref[N_KV_HEADS * g + 1])

    lax.cond(m <= MAX_SAFE_SHIFT, streamed, exact_max)


def _attention(q, kt, ve, kct, vce, qg, kg, ffn_weights, tq):
    b, s, _ = q.shape
    nc = kct.shape[2]
    n_heads = GQA_GROUP * N_KV_HEADS
    assert s % KEY_TILE == 0 and nc % KEY_TILE == 0
    steps = b * (s // tq)
    per_b = s // tq
    flat = [w.reshape(-1, w.shape[-1]) for w in ffn_weights]
    slab_specs = []
    for w in flat:
        every = next(e for e in (1, 2, 4, 8) if steps % e == 0 and w.shape[0] % (steps // e * 16) == 0)
        slab_specs.append(pl.BlockSpec((w.shape[0] // (steps // every), w.shape[1]),
                                       lambda bi, i, every=every: ((bi * per_b + i) // every, 0)))
    out = pl.pallas_call(
        _attn_kernel,
        grid=(b, s // tq),
        in_specs=[
            pl.BlockSpec((1, tq, ATTN_WIDTH), lambda bi, i: (bi, i, 0)),
            pl.BlockSpec((1, KV_WIDTH, s), lambda bi, i: (bi, 0, 0)),
            pl.BlockSpec((1, s, 2 * LANES), lambda bi, i: (bi, 0, 0)),
            pl.BlockSpec((1, KV_WIDTH, nc), lambda bi, i: (bi, 0, 0)),
            pl.BlockSpec((1, nc, 2 * LANES), lambda bi, i: (bi, 0, 0)),
            _const_spec((1, LANES)),
            _const_spec((1, LANES)),
        ] + slab_specs,
        out_specs=[pl.BlockSpec((1, tq, ATTN_WIDTH), lambda bi, i: (bi, i, 0))] + slab_specs,
        out_shape=[jax.ShapeDtypeStruct((b, s, ATTN_WIDTH), BF16)]
        + [jax.ShapeDtypeStruct(w.shape, BF16) for w in flat],
        scratch_shapes=[
            pltpu.VMEM((n_heads, tq, LANES), BF16),
            pltpu.VMEM((n_heads, tq, LANES), F32),
        ],
        compiler_params=_params(2),
        name="attention",
    )(q, kt, ve, kct, vce, qg, kg, *flat)
    return out[0], [wb.reshape(w.shape) for wb, w in zip(out[1:], ffn_weights)]


def _post0_kernel(x_ref, a_ref, gm_ref, mod_ref, gffn_ref, woa_ref, wog_ref, w1_ref, w3_ref, w2_ref,
                  o_ref):
    m = mod_ref[0]
    mix = (jnp.dot(a_ref[0], woa_ref[...], preferred_element_type=F32)
           + jnp.dot(gm_ref[0], wog_ref[...], preferred_element_type=F32))
    x1 = x_ref[0] + m[2:3] * mix
    h = _norm_modulate(x1, gffn_ref[...], m[3:4], m[4:5])
    o_ref[0] = x1 + m[5:6] * _swiglu(h.astype(BF16), w1_ref, w3_ref, w2_ref)


def _post0(x, attn, gm, mods, g_ffn0, woa, wog, w1, w3, w2, layer, tm):
    b, s, d = x.shape
    dff = w1.shape[2]
    return pl.pallas_call(
        _post0_kernel,
        grid=(b, s // tm),
        in_specs=[
            pl.BlockSpec((1, tm, d), lambda bi, i: (bi, i, 0)),
            pl.BlockSpec((1, tm, ATTN_WIDTH), lambda bi, i: (bi, i, 0)),
            pl.BlockSpec((1, tm, GMLP_WIDTH), lambda bi, i: (bi, i, 0)),
            _mod_spec(d, layer, lambda bi, i: bi),
            _layer_spec((1, d), layer),
            _const_spec((ATTN_WIDTH, d)),
            _const_spec((GMLP_WIDTH, d)),
            _layer_spec((d, dff), layer),
            _layer_spec((d, dff), layer),
            _layer_spec((dff, d), layer),
        ],
        out_specs=pl.BlockSpec((1, tm, d), lambda bi, i: (bi, i, 0)),
        out_shape=jax.ShapeDtypeStruct((b, s, d), F32),
        compiler_params=_params(2),
        name="post0",
    )(x, attn, gm, mods, g_ffn0, woa, wog, w1, w3, w2)


def _layer1_kernel(x_ref, prev_ref, next_ref, mod_ref, gmix_ref, gffn_ref, ps_ref, band_ref, wp_ref,
                   w1_ref, w3_ref, w2_ref, gfin_ref, o_ref, *, seq_len):
    tm = x_ref.shape[1]
    i = pl.program_id(1)
    m = mod_ref[0]

    def norm_mod(t):
        return _norm_modulate(t, gmix_ref[...], m[0:1], m[1:2])

    x = x_ref[0]
    xn = norm_mod(x)
    xp = jnp.where(i > 0, norm_mod(prev_ref[0]), 0.0)
    xq = jnp.where(i < pl.num_programs(1) - 1, norm_mod(next_ref[0]), 0.0)
    ext = jnp.concatenate([xp, xn, xq], axis=0)
    ext_hi = ext.astype(BF16)
    ext_lo = (ext - ext_hi.astype(F32)).astype(BF16)

    sub = band_ref.shape[1]
    pos = i * tm + lax.broadcasted_iota(jnp.int32, (tm, LANES), 0)
    ys = []
    for gi, w in enumerate(POOL_WINDOWS):
        left_w = w // 2
        right_w = w - 1 - left_w
        cnt = (jnp.minimum(pos + right_w + 1, seq_len) - jnp.maximum(pos - left_w, 0)).astype(F32)
        inv_cnt = jnp.concatenate([1.0 / cnt] * (POOL_GROUP_DIM // LANES), axis=1)
        sl = slice(gi * POOL_GROUP_DIM, (gi + 1) * POOL_GROUP_DIM)
        sums = []
        for r in range(tm // sub):
            rows = slice(r * sub, r * sub + sub + 2 * HALO)
            hi_lo = jnp.concatenate([ext_hi[rows, sl], ext_lo[rows, sl]], axis=0)
            sums.append(jnp.dot(band_ref[gi], hi_lo, preferred_element_type=F32))
        pooled = jnp.concatenate(sums, axis=0) * inv_cnt - xn[:, sl]
        ys.append(jnp.dot(pooled.astype(BF16), wp_ref[gi], preferred_element_type=F32))
    y = jnp.concatenate(ys, axis=1) * ps_ref[...]
    x1 = x + m[2:3] * y
    h = _norm_modulate(x1, gffn_ref[...], m[3:4], m[4:5])
    x2 = x1 + m[5:6] * _swiglu(h.astype(BF16), w1_ref, w3_ref, w2_ref)
    o_ref[0] = _rms_rows(x2, gfin_ref[...])


def _layer1(x, mods, g_mix1, g_ffn1, pool_scale, band, wp, w1, w3, w2, g_final, layer, tm):
    b, s, d = x.shape
    dff = w1.shape[2]
    per = tm // HALO
    last = s // HALO - 1
    return pl.pallas_call(
        functools.partial(_layer1_kernel, seq_len=s),
        grid=(b, s // tm),
        in_specs=[
            pl.BlockSpec((1, tm, d), lambda bi, i: (bi, i, 0)),
            pl.BlockSpec((1, HALO, d), lambda bi, i: (bi, jnp.maximum(i * per - 1, 0), 0)),
            pl.BlockSpec((1, HALO, d), lambda bi, i: (bi, jnp.minimum((i + 1) * per, last), 0)),
            _mod_spec(d, layer, lambda bi, i: bi),
            _layer_spec((1, d), layer),
            _layer_spec((1, d), layer),
            _const_spec((1, d)),
            _const_spec(band.shape),
            _const_spec(wp.shape),
            _layer_spec((d, dff), layer),
            _layer_spec((d, dff), layer),
            _layer_spec((dff, d), layer),
            _const_spec((1, d)),
        ],
        out_specs=pl.BlockSpec((1, tm, d), lambda bi, i: (bi, i, 0)),
        out_shape=jax.ShapeDtypeStruct((b, s, d), F32),
        compiler_params=_params(2),
        name="layer1",
    )(x, x, x, mods, g_mix1, g_ffn1, pool_scale, band, wp, w1, w3, w2, g_final)


def _rope_tables(n):
    rows = n // GRID_W
    row = np.repeat(np.arange(rows), GRID_W).astype(np.float64)
    col = np.tile(np.arange(GRID_W), rows).astype(np.float64)
    half = HEAD_DIM // 2
    freqs = ROPE_THETA ** (-np.arange(0, half, 2, dtype=np.float64) / half)
    ang = np.concatenate([row[:, None] * freqs, col[:, None] * freqs], axis=-1)
    cos = np.tile(np.repeat(np.cos(ang), 2, axis=1), (1, LANES // HEAD_DIM))
    sin = np.tile(np.repeat(np.sin(ang), 2, axis=1), (1, LANES // HEAD_DIM))
    sign = np.where(np.arange(LANES) % 2 == 0, -1.0, 1.0)
    return jnp.asarray(cos, dtype=F32), jnp.asarray(sin * sign, dtype=F32)


def _band_matrices(sub):
    t = np.arange(sub)[:, None]
    e = np.arange(sub + 2 * HALO)[None, :]
    mats = []
    for w in POOL_WINDOWS:
        left = w // 2
        right = w - 1 - left
        member = ((e >= t + HALO - left) & (e <= t + HALO + right)).astype(np.float32)
        mats.append(np.concatenate([member, member], axis=1))
    return jnp.asarray(np.stack(mats), dtype=BF16)


def kernel(x, c, ctx, c_ctx, w_ada, b_ada, g_mix, g_ffn, w_in, w_out, q_norm, k_norm, gmlp_norm,
           w_spatial, b_spatial, w_pool, pool_scale, w1, w3, w2, g_final):
    b, s, d = x.shape
    depth = w_ada.shape[0]
    assert depth == 2 and d == D_MODEL and s % CHUNK == 0
    tm_pre, tq, tm_ffn = 1024, 512, 1024

    cond = jnp.concatenate([c, c_ctx[None], jnp.zeros((COND_ROWS - b - 1, d), F32)], axis=0)
    mods = _adaln(cond, w_ada, b_ada).reshape(depth, COND_ROWS, 6, d)

    wi = w_in[0]
    wq = wi[:, :ATTN_WIDTH].reshape(d, N_KV_HEADS, GQA_GROUP, HEAD_DIM).transpose(0, 2, 1, 3)
    w_in_b = jnp.concatenate([wq.reshape(d, ATTN_WIDTH), wi[:, ATTN_WIDTH:]], axis=1).astype(BF16)
    w_kv_b = wi[:, ATTN_WIDTH:ATTN_WIDTH + 2 * KV_WIDTH].astype(BF16)
    wo = w_out[0]
    woa = wo[:ATTN_WIDTH].reshape(N_KV_HEADS, GQA_GROUP, HEAD_DIM, d).transpose(1, 0, 2, 3)
    woa = woa.reshape(ATTN_WIDTH, d).astype(BF16)
    wog = wo[ATTN_WIDTH:].astype(BF16)
    cos_t, sin_t = _rope_tables(s)
    qg = jnp.tile(q_norm[0], LANES // HEAD_DIM)[None]
    kg = jnp.tile(k_norm[0], LANES // HEAD_DIM)[None]
    gv = gmlp_norm[0].reshape(1, GMLP_WIDTH)
    seg = np.arange(2 * LANES) // HEAD_DIM
    bsum = jnp.asarray((seg[:, None] == seg[None, :]).astype(np.float32), dtype=BF16)
    ws = w_spatial[0]
    ws2 = jnp.concatenate([ws[0::2], ws[1::2]], axis=2).astype(BF16)
    bs_t = jnp.repeat(b_spatial[0].T, GMLP_GROUP_DIM, axis=1)

    g_mix3, g_ffn3 = g_mix.reshape(depth, 1, d), g_ffn.reshape(depth, 1, d)

    q, kt, v, gm = _pre0(x, mods, g_mix3, w_in_b, cos_t, sin_t, qg, kg, gv, bsum, ws2, bs_t, tm_pre)
    kct, vc = _ctx_kv(ctx, mods, g_mix3, w_kv_b, kg, bsum, b, CTX_BATCHES_PER_STEP)
    attn, (w1b, w3b, w2b) = _attention(q, kt, v, kct, vc, qg, kg, (w1, w3, w2), tq)
    x1 = _post0(x, attn, gm, mods, g_ffn3, woa, wog, w1b, w3b, w2b, 0, tm_ffn)

    band = _band_matrices(POOL_SUB)
    return _layer1(x1, mods, g_mix3, g_ffn3, pool_scale[:1], band,
                   w_pool[0].astype(BF16), w1b, w3b, w2b, g_final[None], 1, tm_ffn)
```
